```python
import jax, jax.numpy as jnp
from jax import lax
import numpy as np

D_MODEL = 2048
BATCH = 2
SEQ = 4096
DEPTH = 4

N_MIXERS = 2
N_A = (DEPTH + 1) // 2
N_B = DEPTH // 2
MEM_LEN = 256
EPS = 1e-6
CONV_W = 4

D_RNN = ((4 * D_MODEL // 3 + 128) // 256) * 256
LRU_BLOCKS = 16
LRU_BLOCK_DIM = D_RNN // LRU_BLOCKS
LRU_C = 8.0

GDN_QK_HEADS = 16
GDN_V_HEADS = 32
GDN_HEAD_DIM = 128
GDN_D_QK = GDN_QK_HEADS * GDN_HEAD_DIM
GDN_D_V = GDN_V_HEADS * GDN_HEAD_DIM
GDN_CHUNK = 64

X_HEADS = 4
X_HEAD_DIM = D_MODEL // X_HEADS

D_FF = ((8 * D_MODEL // 3 + 255) // 256) * 256

LRU_IN = 2 * D_RNN + D_MODEL
LRU_OUT_IN = D_RNN + D_MODEL
GDN_IN = 2 * GDN_D_QK + 2 * GDN_D_V + 2 * GDN_V_HEADS + D_MODEL
GDN_OUT_IN = GDN_D_V + D_MODEL

kernel_name = "hybrid_rglru_gated_deltanet_memxattn_swiglu"


def rms_norm(x, g):
    xf = x.astype(jnp.float32)
    y = xf * lax.rsqrt(jnp.mean(xf * xf, axis=-1, keepdims=True) + EPS)
    return (y * g.astype(jnp.float32)).astype(x.dtype)


def causal_depthwise_conv(x, w):
    k_w, c = w.shape
    return lax.conv_general_dilated(
        x, w[:, None, :].astype(x.dtype), window_strides=(1,), padding=[(k_w - 1, 0)],
        dimension_numbers=("NWC", "WIO", "NWC"), feature_group_count=c)


def l2_normalize(t):
    return t * lax.rsqrt(jnp.sum(t * t, axis=-1, keepdims=True) + EPS)


def memory_cross_attention(q, mem_n, w_mem_kv):
    b, s, _ = q.shape
    m = mem_n.shape[1]
    k, v = jnp.split(mem_n @ w_mem_kv, 2, axis=-1)
    q = q.reshape(b, s, X_HEADS, X_HEAD_DIM)
    k = k.reshape(b, m, X_HEADS, X_HEAD_DIM)
    v = v.reshape(b, m, X_HEADS, X_HEAD_DIM)
    scores = jnp.einsum("bshd,bmhd->bhsm", q, k).astype(jnp.float32) * (X_HEAD_DIM ** -0.5)
    p = jax.nn.softmax(scores, axis=-1).astype(v.dtype)
    o = jnp.einsum("bhsm,bmhd->bshd", p, v)
    return o.reshape(b, s, D_MODEL)


def rglru_mixer(u, conv_w, conv_b, gate_a_w, gate_a_b, gate_x_w, gate_x_b, lam):
    b, s, _ = u.shape
    xb, gb = jnp.split(u, 2, axis=-1)
    xb = causal_depthwise_conv(xb, conv_w) + conv_b
    xblk = xb.reshape(b, s, LRU_BLOCKS, LRU_BLOCK_DIM)
    r = jax.nn.sigmoid(jnp.einsum("bshi,hij->bshj", xblk, gate_a_w).reshape(b, s, D_RNN) + gate_a_b)
    i = jax.nn.sigmoid(jnp.einsum("bshi,hij->bshj", xblk, gate_x_w).reshape(b, s, D_RNN) + gate_x_b)
    log_a = -LRU_C * r.astype(jnp.float32) * jax.nn.softplus(-lam.astype(jnp.float32))
    a = jnp.exp(log_a)
    b_in = jnp.sqrt(-jnp.expm1(2.0 * log_a)) * (i * xb).astype(jnp.float32)

    def combine(left, right):
        a1, b1 = left
        a2, b2 = right
        return a1 * a2, a2 * b1 + b2

    _, h = lax.associative_scan(combine, (a, b_in), axis=1)
    return (h * jax.nn.gelu(gb.astype(jnp.float32))).astype(u.dtype)


def chunk_gated_delta_rule(q, k, v, g, beta):
    b, s, h, dk = q.shape
    dv = v.shape[-1]
    c = GDN_CHUNK
    n = s // c

    def to_chunks(t):
        return t.reshape(b, n, c, h, *t.shape[3:]).swapaxes(2, 3)

    q, k, v, g, beta = (to_chunks(t) for t in (q, k, v, g, beta))
    g = jnp.cumsum(g, axis=-1)
    kb = k * beta[..., None]
    vb = v * beta[..., None]
    causal = jnp.tril(jnp.ones((c, c), dtype=bool))
    strict = jnp.tril(jnp.ones((c, c), dtype=bool), k=-1)
    diff = g[..., :, None] - g[..., None, :]
    decay = jnp.where(causal, jnp.exp(jnp.where(causal, diff, 0.0)), 0.0)
    a_mat = jnp.where(strict, jnp.einsum("bnhid,bnhjd->bnhij", kb, k) * decay, 0.0)
    eye = jnp.eye(c, dtype=jnp.float32)
    t_inv = lax.linalg.triangular_solve(eye + a_mat, jnp.broadcast_to(eye, a_mat.shape),
                                        left_side=True, lower=True, unit_diagonal=True)
    w_vals = jnp.einsum("bnhij,bnhjd->bnhid", t_inv, vb)
    k_cum = jnp.einsum("bnhij,bnhjd->bnhid", t_inv, kb * jnp.exp(g)[..., None])
    attn_intra = jnp.where(causal, jnp.einsum("bnhid,bnhjd->bnhij", q, k) * decay, 0.0)
    g_last = g[..., -1]
    k_state = k * jnp.exp(g_last[..., None] - g)[..., None]
    q_dec = q * jnp.exp(g)[..., None]

    def step(state, xs):
        qd, kc, wv, att, ks, gl = xs
        v_new = wv - jnp.einsum("bhcd,bhde->bhce", kc, state)
        o = jnp.einsum("bhcd,bhde->bhce", qd, state) + jnp.einsum("bhij,bhje->bhie", att, v_new)
        state = state * jnp.exp(gl)[..., None, None] + jnp.einsum("bhcd,bhce->bhde", ks, v_new)
        return state, o

    xs = tuple(jnp.moveaxis(t, 1, 0) for t in (q_dec, k_cum, w_vals, attn_intra, k_state, g_last))
    s0 = jnp.zeros((b, h, dk, dv), jnp.float32)
    _, o = lax.scan(step, s0, xs)
    return jnp.moveaxis(o, 0, 1).swapaxes(2, 3).reshape(b, s, h, dv)


def gated_deltanet_mixer(u, conv_w, a_log, dt_bias, o_norm_g):
    b, s, _ = u.shape
    split_at = [GDN_D_QK + GDN_D_QK + GDN_D_V, GDN_D_QK + GDN_D_QK + 2 * GDN_D_V,
                GDN_D_QK + GDN_D_QK + 2 * GDN_D_V + GDN_V_HEADS]
    qkv, z, beta_in, alpha_in = jnp.split(u, split_at, axis=-1)
    qkv = jax.nn.silu(causal_depthwise_conv(qkv, conv_w))
    q, k, v = jnp.split(qkv, [GDN_D_QK, 2 * GDN_D_QK], axis=-1)
    rep = GDN_V_HEADS // GDN_QK_HEADS
    q = l2_normalize(q.astype(jnp.float32).reshape(b, s, GDN_QK_HEADS, GDN_HEAD_DIM))
    k = l2_normalize(k.astype(jnp.float32).reshape(b, s, GDN_QK_HEADS, GDN_HEAD_DIM))
    q = jnp.repeat(q, rep, axis=2) * (GDN_HEAD_DIM ** -0.5)
    k = jnp.repeat(k, rep, axis=2)
    v = v.astype(jnp.float32).reshape(b, s, GDN_V_HEADS, GDN_HEAD_DIM)
    beta = jax.nn.sigmoid(beta_in.astype(jnp.float32))
    g = -jnp.exp(a_log.astype(jnp.float32)) * jax.nn.softplus(
        alpha_in.astype(jnp.float32) + dt_bias.astype(jnp.float32))
    o = chunk_gated_delta_rule(q, k, v, g, beta)
    z = z.astype(jnp.float32).reshape(b, s, GDN_V_HEADS, GDN_HEAD_DIM)
    o = rms_norm(o, o_norm_g) * jax.nn.silu(z)
    return o.reshape(b, s, GDN_D_V).astype(u.dtype)


def setup_inputs(seed: int = 0) -> dict:
    key = jax.random.key(seed)
    ks = jax.random.split(key, 26)
    f32 = jnp.float32

    def nrm(k, shape, fan_in):
        return jax.random.normal(k, shape, f32) * (fan_in ** -0.5)

    def gain(k, shape):
        return 1.0 + 0.02 * jax.random.normal(k, shape, f32)

    u = jax.random.uniform(ks[16], (N_A, D_RNN), f32, 0.9, 0.999)
    sig = u ** (1.0 / LRU_C)
    lru_lambda = jnp.log(sig) - jnp.log1p(-sig)
    dt = jnp.exp(jax.random.uniform(ks[21], (N_B, GDN_V_HEADS), f32, np.log(1e-3), np.log(1e-1)))
    gdn_dt_bias = dt + jnp.log(-jnp.expm1(-dt))
    return {
        "x": jax.random.normal(ks[0], (BATCH, SEQ, D_MODEL), f32),
        "mem": jax.random.normal(ks[1], (BATCH, MEM_LEN, D_MODEL), f32),
        "norm_mix_g": gain(ks[2], (DEPTH, D_MODEL)),
        "norm_mem_g": gain(ks[3], (DEPTH, D_MODEL)),
        "mem_kv_w": nrm(ks[4], (DEPTH, D_MODEL, 2 * D_MODEL), D_MODEL),
        "norm_ffn_g": gain(ks[5], (DEPTH, D_MODEL)),
        "ffn_w_in": nrm(ks[6], (DEPTH, D_MODEL, 2 * D_FF), D_MODEL),
        "ffn_w_out": nrm(ks[7], (DEPTH, D_FF, D_MODEL), D_FF),
        "lru_w_in": nrm(ks[8], (N_A, D_MODEL, LRU_IN), D_MODEL),
        "lru_w_out": nrm(ks[9], (N_A, LRU_OUT_IN, D_MODEL), LRU_OUT_IN),
        "lru_conv_w": nrm(ks[10], (N_A, CONV_W, D_RNN), CONV_W),
        "lru_conv_b": 0.01 * jax.random.normal(ks[11], (N_A, D_RNN), f32),
        "lru_gate_a_w": nrm(ks[12], (N_A, LRU_BLOCKS, LRU_BLOCK_DIM, LRU_BLOCK_DIM), LRU_BLOCK_DIM),
        "lru_gate_a_b": 0.01 * jax.random.normal(ks[13], (N_A, D_RNN), f32),
        "lru_gate_x_w": nrm(ks[14], (N_A, LRU_BLOCKS, LRU_BLOCK_DIM, LRU_BLOCK_DIM), LRU_BLOCK_DIM),
        "lru_gate_x_b": 0.01 * jax.random.normal(ks[15], (N_A, D_RNN), f32),
        "lru_lambda": lru_lambda,
        "gdn_w_in": nrm(ks[17], (N_B, D_MODEL, GDN_IN), D_MODEL),
        "gdn_w_out": nrm(ks[18], (N_B, GDN_OUT_IN, D_MODEL), GDN_OUT_IN),
        "gdn_conv_w": nrm(ks[19], (N_B, CONV_W, 2 * GDN_D_QK + GDN_D_V), CONV_W),
        "gdn_a_log": jnp.log(jax.random.uniform(ks[20], (N_B, GDN_V_HEADS), f32, 1.0, 16.0)),
        "gdn_dt_bias": gdn_dt_bias,
        "gdn_norm_g": gain(ks[22], (N_B, GDN_HEAD_DIM)),
        "final_norm_g": gain(ks[23], (D_MODEL,)),
    }


def reference(x, mem, norm_mix_g, norm_mem_g, mem_kv_w, norm_ffn_g, ffn_w_in, ffn_w_out,
              lru_w_in, lru_w_out, lru_conv_w, lru_conv_b, lru_gate_a_w, lru_gate_a_b,
              lru_gate_x_w, lru_gate_x_b, lru_lambda,
              gdn_w_in, gdn_w_out, gdn_conv_w, gdn_a_log, gdn_dt_bias, gdn_norm_g,
              final_norm_g):
    h = x
    for i in range(DEPTH):
        hn = rms_norm(h, norm_mix_g[i])
        mem_n = rms_norm(mem, norm_mem_g[i])
        j = i // N_MIXERS
        if i % N_MIXERS == 0:
            proj = hn @ lru_w_in[j]
            mix_in, xq = proj[..., :2 * D_RNN], proj[..., 2 * D_RNN:]
            y = rglru_mixer(mix_in, lru_conv_w[j], lru_conv_b[j], lru_gate_a_w[j], lru_gate_a_b[j],
                            lru_gate_x_w[j], lru_gate_x_b[j], lru_lambda[j])
            w_out = lru_w_out[j]
        else:
            proj = hn @ gdn_w_in[j]
            mix_in, xq = proj[..., :GDN_IN - D_MODEL], proj[..., GDN_IN - D_MODEL:]
            y = gated_deltanet_mixer(mix_in, gdn_conv_w[j], gdn_a_log[j], gdn_dt_bias[j], gdn_norm_g[j])
            w_out = gdn_w_out[j]
        xo = memory_cross_attention(xq, mem_n, mem_kv_w[i])
        h = h + jnp.concatenate([y, xo], axis=-1) @ w_out
        hn = rms_norm(h, norm_ffn_g[i])
        gate, up = jnp.split(hn @ ffn_w_in[i], 2, axis=-1)
        h = h + (jax.nn.silu(gate) * up) @ ffn_w_out[i]
    return rms_norm(h, final_norm_g)
```

```python
import functools

import jax
import jax.numpy as jnp
from jax import lax
from jax.experimental import pallas as pl
from jax.experimental.pallas import tpu as pltpu

F32 = jnp.float32
BF = jnp.bfloat16

EPS = 1e-6
CONV_W = 4
LRU_BLOCKS = 16
LRU_C = 8.0
GDN_HEAD_DIM = 128
GDN_CHUNK = 64
X_HEADS = 4

LANES = 128
SUBLANES = 8
VMEM_LIMIT = 56 * 1024 * 1024


def _params(n_axes):
    return pltpu.CompilerParams(
        dimension_semantics=("arbitrary",) * n_axes, vmem_limit_bytes=VMEM_LIMIT)


def _rmsnorm_kernel(x_ref, g_ref, o_ref):
    x = x_ref[...]
    ms = jnp.mean(x * x, axis=-1, keepdims=True)
    o_ref[...] = (x * lax.rsqrt(ms + EPS) * g_ref[...]).astype(o_ref.dtype)


def _rmsnorm(x, g, out_dtype, tm=512):
    m, d = x.shape
    tm = min(tm, m)
    return pl.pallas_call(
        _rmsnorm_kernel,
        grid=(m // tm,),
        in_specs=[pl.BlockSpec((tm, d), lambda i: (i, 0)),
                  pl.BlockSpec((1, d), lambda i: (0, 0))],
        out_specs=pl.BlockSpec((tm, d), lambda i: (i, 0)),
        out_shape=jax.ShapeDtypeStruct((m, d), out_dtype),
        compiler_params=_params(1),
        name="rmsnorm",
    )(x, g.reshape(1, d))


_CAST_ROWS = 512


def _cast_weight(w_ref, wb_ref):
    k = w_ref.shape[0]
    rows = _CAST_ROWS if k % _CAST_ROWS == 0 else k

    def body(r, carry):
        r0 = pl.multiple_of(r * rows, rows)
        wb_ref[pl.ds(r0, rows), :] = w_ref[pl.ds(r0, rows), :].astype(BF)
        return carry

    lax.fori_loop(0, k // rows, body, 0)


def _mm_kernel(*refs, k_sizes, has_res):
    n_x = len(k_sizes)
    x_refs = refs[:n_x]
    w_ref = refs[n_x]
    res_ref = refs[n_x + 1] if has_res else None
    o_ref = refs[n_x + 1 + has_res]
    wb_ref = refs[n_x + 2 + has_res]

    @pl.when(pl.program_id(1) == 0)
    def _():
        _cast_weight(w_ref, wb_ref)

    acc = None
    off = 0
    for x_ref, k in zip(x_refs, k_sizes):
        part = jnp.dot(x_ref[...], wb_ref[off:off + k, :], preferred_element_type=F32)
        acc = part if acc is None else acc + part
        off += k
    if has_res:
        acc = acc + res_ref[...]
    o_ref[...] = acc.astype(o_ref.dtype)


def _matmul(xs, w, col0, n_cols, out_dtype, residual=None, tm=512, tn=512):
    m = xs[0].shape[0]
    k_sizes = tuple(x.shape[1] for x in xs)
    k_tot = sum(k_sizes)
    assert w.shape[0] == k_tot
    tm = min(tm, m)
    tn = min(tn, n_cols)
    assert m % tm == 0 and n_cols % tn == 0 and col0 % tn == 0
    cb0 = col0 // tn
    in_specs = [pl.BlockSpec((tm, k), lambda j, i: (i, 0)) for k in k_sizes]
    in_specs.append(pl.BlockSpec((k_tot, tn), lambda j, i: (0, j + cb0)))
    args = list(xs) + [w]
    if residual is not None:
        in_specs.append(pl.BlockSpec((tm, tn), lambda j, i: (i, j)))
        args.append(residual)
    return pl.pallas_call(
        functools.partial(_mm_kernel, k_sizes=k_sizes, has_res=residual is not None),
        grid=(n_cols // tn, m // tm),
        in_specs=in_specs,
        out_specs=pl.BlockSpec((tm, tn), lambda j, i: (i, j)),
        out_shape=jax.ShapeDtypeStruct((m, n_cols), out_dtype),
        scratch_shapes=[pltpu.VMEM((k_tot, tn), BF)],
        compiler_params=_params(2),
        name="matmul",
    )(*args)


def _ffn_in_kernel(x_ref, wg_ref, wu_ref, o_ref, wgb_ref, wub_ref):
    @pl.when(pl.program_id(1) == 0)
    def _():
        _cast_weight(wg_ref, wgb_ref)
        _cast_weight(wu_ref, wub_ref)

    x = x_ref[...]
    gate = jnp.dot(x, wgb_ref[...], preferred_element_type=F32)
    up = jnp.dot(x, wub_ref[...], preferred_element_type=F32)
    o_ref[...] = (gate * jax.nn.sigmoid(gate) * up).astype(o_ref.dtype)


def _ffn_in(x, w, tm=1024, tn=512):
    m, k = x.shape
    f = w.shape[1] // 2
    assert m % tm == 0 and f % tn == 0
    nb = f // tn
    return pl.pallas_call(
        _ffn_in_kernel,
        grid=(nb, m // tm),
        in_specs=[pl.BlockSpec((tm, k), lambda j, i: (i, 0)),
                  pl.BlockSpec((k, tn), lambda j, i: (0, j)),
                  pl.BlockSpec((k, tn), lambda j, i: (0, j + nb))],
        out_specs=pl.BlockSpec((tm, tn), lambda j, i: (i, j)),
        out_shape=jax.ShapeDtypeStruct((m, f), BF),
        scratch_shapes=[pltpu.VMEM((k, tn), BF), pltpu.VMEM((k, tn), BF)],
        compiler_params=_params(2),
        name="ffn_in",
    )(x, w, w)


def _xattn_kernel(q_ref, k_ref, v_ref, o_ref, *, heads, hd):
    scale = hd ** -0.5
    for h in range(heads):
        sl = slice(h * hd, (h + 1) * hd)
        s = lax.dot_general(q_ref[:, sl], k_ref[:, sl], (((1,), (1,)), ((), ())),
                            preferred_element_type=F32) * scale
        mx = jnp.max(s, axis=-1, keepdims=True)
        p = jnp.exp(s - mx)
        denom = jnp.sum(p, axis=-1, keepdims=True)
        o = jnp.dot(p.astype(BF), v_ref[:, sl], preferred_element_type=F32)
        o_ref[:, sl] = (o / denom).astype(o_ref.dtype)


def _xattn(q, kv, ts=512):
    b, s, d = q.shape
    mem = kv.shape[1]
    return pl.pallas_call(
        functools.partial(_xattn_kernel, heads=X_HEADS, hd=d // X_HEADS),
        grid=(b, s // ts),
        in_specs=[pl.BlockSpec((None, ts, d), lambda bi, si: (bi, si, 0)),
                  pl.BlockSpec((None, mem, d), lambda bi, si: (bi, 0, 0)),
                  pl.BlockSpec((None, mem, d), lambda bi, si: (bi, 0, 1))],
        out_specs=pl.BlockSpec((None, ts, d), lambda bi, si: (bi, si, 0)),
        out_shape=jax.ShapeDtypeStruct((b, s, d), BF),
        compiler_params=_params(2),
        name="xattn",
    )(q, kv, kv)


def _causal_conv(x, tail, cw_ref, cols):
    t = x.shape[0]
    row8 = lax.broadcasted_iota(jnp.int32, (SUBLANES, x.shape[1]), 0)
    acc = x * cw_ref[CONV_W - 1:CONV_W, cols]
    for d in range(1, CONV_W):
        xr = pltpu.roll(x, d, axis=0)
        head = jnp.where(row8 < d, pltpu.roll(tail, d, axis=0), xr[0:SUBLANES])
        xs = jnp.concatenate([head, xr[SUBLANES:t]], axis=0)
        acc = acc + xs * cw_ref[CONV_W - 1 - d:CONV_W - d, cols]
    return acc


def _band_layout(c, tn, blk):
    starts, need = [], 0
    for j in range(c // tn):
        lo = (j * tn // blk) * blk
        hi = (((j + 1) * tn - 1) // blk + 1) * blk
        st = (lo // LANES) * LANES
        starts.append(st)
        need = max(need, hi - st)
    kw = -(-need // LANES) * LANES
    starts = [min(st, c - kw) for st in starts]
    return tuple(starts), kw


def _band_weights(gw, starts, kw, tn):
    dense = jax.scipy.linalg.block_diag(*[gw[i] for i in range(gw.shape[0])])
    return jnp.stack([dense[st:st + kw, j * tn:(j + 1) * tn]
                      for j, st in enumerate(starts)]).astype(BF)


def _scan_linear(a, b):
    t = a.shape[0]
    row = lax.broadcasted_iota(jnp.int32, a.shape, 0)
    d = 1
    while d < SUBLANES:
        keep = row >= d
        a_s = jnp.where(keep, pltpu.roll(a, d, axis=0), 1.0)
        b_s = jnp.where(keep, pltpu.roll(b, d, axis=0), 0.0)
        b = a * b_s + b
        a = a * a_s
        d *= 2
    while d < t:
        b = jnp.concatenate([b[:d], a[d:] * b[:t - d] + b[d:]], axis=0)
        a = jnp.concatenate([a[:d], a[d:] * a[:t - d]], axis=0)
        d *= 2
    return a, b


def _rglru_kernel(xb_ref, gb_ref, cw_ref, cb_ref, wa_ref, wx_ref, ba_ref, bx_ref, lam_ref,
                  y_ref, tail_ref, hc_ref, xc_ref, xcb_ref, *, starts, kw, tn):
    t, c = xb_ref.shape

    @pl.when(pl.program_id(1) == 0)
    def _():
        tail_ref[...] = jnp.zeros_like(tail_ref)
        hc_ref[...] = jnp.zeros_like(hc_ref)

    for j in range(c // tn):
        cols = slice(j * tn, (j + 1) * tn)
        x = xb_ref[:, cols].astype(F32)
        xc = _causal_conv(x, tail_ref[:, cols], cw_ref, cols) + cb_ref[:, cols]
        tail_ref[:, cols] = x[t - SUBLANES:t]
        xc_ref[:, cols] = xc
        xcb_ref[:, cols] = xc.astype(BF)

    for j, st in enumerate(starts):
        cols = slice(j * tn, (j + 1) * tn)
        xw = xcb_ref[:, st:st + kw]
        r = jax.nn.sigmoid(jnp.dot(xw, wa_ref[j], preferred_element_type=F32) + ba_ref[:, cols])
        i = jax.nn.sigmoid(jnp.dot(xw, wx_ref[j], preferred_element_type=F32) + bx_ref[:, cols])
        neg_lam = -lam_ref[:, cols]
        softplus = jnp.maximum(neg_lam, 0.0) + jnp.log1p(jnp.exp(-jnp.abs(neg_lam)))
        log_a = (-LRU_C) * r * softplus
        a = jnp.exp(log_a)
        th = jnp.tanh(log_a)
        b_in = jnp.sqrt(-2.0 * th / (1.0 - th)) * (i * xc_ref[:, cols])
        a_cum, h_loc = _scan_linear(a, b_in)
        h = h_loc + a_cum * hc_ref[0:1, cols]
        hc_ref[:, cols] = jnp.broadcast_to(h[t - 1:t], (SUBLANES, tn))
        y_ref[:, cols] = (h * jax.nn.gelu(gb_ref[:, cols].astype(F32))).astype(y_ref.dtype)


def _rglru(proj, conv_w, conv_b, gate_a_w, gate_a_b, gate_x_w, gate_x_b, lam, c, t=128, tn=256):
    b, s, _ = proj.shape
    starts, kw = _band_layout(c, tn, c // LRU_BLOCKS)
    wa = _band_weights(gate_a_w, starts, kw, tn)
    wx = _band_weights(gate_x_w, starts, kw, tn)
    nt = c // tn
    row = lambda v: v.reshape(1, c)
    full = lambda shape: pl.BlockSpec(shape, lambda bi, ti: (0,) * len(shape))
    return pl.pallas_call(
        functools.partial(_rglru_kernel, starts=starts, kw=kw, tn=tn),
        grid=(b, s // t),
        in_specs=[pl.BlockSpec((None, t, c), lambda bi, ti: (bi, ti, 0)),
                  pl.BlockSpec((None, t, c), lambda bi, ti: (bi, ti, 1)),
                  full((CONV_W, c)), full((1, c)),
                  full((nt, kw, tn)), full((nt, kw, tn)),
                  full((1, c)), full((1, c)), full((1, c))],
        out_specs=pl.BlockSpec((None, t, c), lambda bi, ti: (bi, ti, 0)),
        out_shape=jax.ShapeDtypeStruct((b, s, c), BF),
        scratch_shapes=[pltpu.VMEM((SUBLANES, c), F32), pltpu.VMEM((SUBLANES, c), F32),
                        pltpu.VMEM((t, c), F32), pltpu.VMEM((t, c), BF)],
        compiler_params=_params(2),
        name="rglru",
    )(proj, proj, conv_w, row(conv_b), wa, wx, row(gate_a_b), row(gate_x_b), row(lam))


def _gdn_prep_kernel(x_ref, ba_ref, cw_ref, alog_ref, dtb_ref, o_ref, bg_ref, tail_ref,
                     *, d_qk, n_heads, tc):
    t, c = x_ref.shape

    @pl.when(pl.program_id(1) == 0)
    def _():
        tail_ref[...] = jnp.zeros_like(tail_ref)

    hd = GDN_HEAD_DIM
    for j in range(c // tc):
        cols = slice(j * tc, (j + 1) * tc)
        x = x_ref[:, cols].astype(F32)
        y = _causal_conv(x, tail_ref[:, cols], cw_ref, cols)
        tail_ref[:, cols] = x[t - SUBLANES:t]
        y = y * jax.nn.sigmoid(y)
        if j * tc < 2 * d_qk:
            scale = hd ** -0.5 if j * tc < d_qk else 1.0
            for hh in range(tc // hd):
                yh = y[:, hh * hd:(hh + 1) * hd]
                ss = jnp.sum(yh * yh, axis=-1, keepdims=True)
                yn = yh * lax.rsqrt(ss + EPS)
                if scale != 1.0:
                    yn = yn * scale
                o_ref[:, j * tc + hh * hd:j * tc + (hh + 1) * hd] = yn.astype(o_ref.dtype)
        else:
            o_ref[:, cols] = y.astype(o_ref.dtype)

    ba = ba_ref[...]
    lane = lax.broadcasted_iota(jnp.int32, ba.shape, 1)
    z = ba + dtb_ref[...]
    softplus = jnp.maximum(z, 0.0) + jnp.log1p(jnp.exp(-jnp.abs(z)))
    g = -jnp.exp(alog_ref[...]) * softplus
    bg_ref[...] = jnp.where(lane < n_heads, jax.nn.sigmoid(ba), g)


def _gdn_prep(proj, ba, conv_w, a_log, dt_bias, d_qk, d_v, t=256, tc=512):
    b, s, _ = proj.shape
    c = 2 * d_qk + d_v
    nh = a_log.shape[0]
    pad = lambda v: jnp.zeros((1, LANES), F32).at[0, nh:2 * nh].set(v)
    full = lambda shape: pl.BlockSpec(shape, lambda bi, ti: (0,) * len(shape))
    return pl.pallas_call(
        functools.partial(_gdn_prep_kernel, d_qk=d_qk, n_heads=nh, tc=tc),
        grid=(b, s // t),
        in_specs=[pl.BlockSpec((None, t, c), lambda bi, ti: (bi, ti, 0)),
                  pl.BlockSpec((None, t, LANES), lambda bi, ti: (bi, ti, 0)),
                  full((CONV_W, c)), full((1, LANES)), full((1, LANES))],
        out_specs=[pl.BlockSpec((None, t, c), lambda bi, ti: (bi, ti, 0)),
                   pl.BlockSpec((None, t, LANES), lambda bi, ti: (bi, ti, 0))],
        out_shape=[jax.ShapeDtypeStruct((b, s, c), BF),
                   jax.ShapeDtypeStruct((b, s, LANES), F32)],
        scratch_shapes=[pltpu.VMEM((SUBLANES, c), F32)],
        compiler_params=_params(2),
        name="gdn_prep",
    )(proj, ba, conv_w, pad(a_log), pad(dt_bias))


def _split3(x):
    hi = x.astype(BF)
    r1 = x - hi.astype(F32)
    mid = r1.astype(BF)
    lo = (r1 - mid.astype(F32)).astype(BF)
    return hi, mid, lo


def _dot_precise(a, b):
    a_hi, a_mid, _ = _split3(a)
    b_hi, b_mid, _ = _split3(b)
    dot = functools.partial(jnp.dot, preferred_element_type=F32)
    return dot(a_hi, b_hi) + (dot(a_hi, b_mid) + dot(a_mid, b_hi))


def _block_diag2(x, half):
    lane = lax.broadcasted_iota(jnp.int32, x.shape, 1)
    zero = jnp.zeros_like(x)
    return jnp.concatenate([jnp.where(lane < half, x, zero), jnp.where(lane >= half, x, zero)], axis=0)


def _gdn_delta_kernel(q_ref, k_ref, v_ref, z_ref, bg_ref, ng_ref, o_ref, state_ref, *, n_heads):
    ck = GDN_CHUNK
    hd = GDN_HEAD_DIM
    ts = q_ref.shape[0]
    pair = pl.program_id(1)

    @pl.when(pl.program_id(2) == 0)
    def _():
        state_ref[...] = jnp.zeros_like(state_ref)

    lane64 = lax.broadcasted_iota(jnp.int32, (ck, 2 * ck), 1)
    row64 = lax.broadcasted_iota(jnp.int32, (ck, 2 * ck), 0)
    col_in_head = jnp.where(lane64 < ck, lane64, lane64 - ck)
    first64 = lane64 < ck
    causal = row64 >= col_in_head
    strict = row64 > col_in_head
    eye = row64 == col_in_head
    lane128 = lax.broadcasted_iota(jnp.int32, (ck, 2 * hd), 1)
    first128 = lane128 < hd
    lane_bg = lax.broadcasted_iota(jnp.int32, (ck, LANES), 1)
    h0 = 2 * pair
    dot = functools.partial(jnp.dot, preferred_element_type=F32)

    def chunk(ci, carry):
        r0 = pl.multiple_of(ci * ck, ck)
        rows = pl.ds(r0, ck)
        bg = bg_ref[rows, :]

        def column(l):
            return jnp.sum(jnp.where(lane_bg == l, bg, 0.0), axis=-1, keepdims=True)

        beta0, beta1 = column(h0), column(h0 + 1)
        g0, g1 = column(n_heads + h0), column(n_heads + h0 + 1)

        gc = jnp.where(first64, g0, g1)
        d = 1
        while d < ck:
            if d < SUBLANES:
                shifted = jnp.where(row64 >= d, pltpu.roll(gc, d, axis=0), 0.0)
                gc = gc + shifted
            else:
                gc = jnp.concatenate([gc[:d], gc[d:] + gc[:ck - d]], axis=0)
            d *= 2
        g_row = jnp.sum(jnp.where(eye, gc, 0.0), axis=0, keepdims=True)
        diff = gc - g_row
        decay = jnp.where(causal, jnp.exp(jnp.where(causal, diff, 0.0)), 0.0)

        q = q_ref[rows, :]
        k = k_ref[rows, :]
        qk = jnp.concatenate([q, k], axis=0)
        kk = jnp.concatenate([k, k], axis=0)
        prod = lax.dot_general(qk, kk, (((1,), (1,)), ((), ())), preferred_element_type=F32)
        beta64 = jnp.where(first64, beta0, beta1)
        a_mat = jnp.where(strict, prod[ck:] * beta64 * decay, 0.0)
        attn = jnp.where(causal, prod[:ck] * decay, 0.0)

        p_mat = jnp.where(eye, 1.0, 0.0) - a_mat
        m_mat = _dot_precise(a_mat, _block_diag2(a_mat, ck))
        n_sq = 1
        while (2 << n_sq) < ck:
            n_sq += 1
        for step in range(n_sq):
            bd = _block_diag2(m_mat, ck)
            if step + 1 < n_sq:
                both = _dot_precise(jnp.concatenate([p_mat, m_mat], axis=0), bd)
                p_mat = p_mat + both[:ck]
                m_mat = both[ck:]
            else:
                p_mat = p_mat + _dot_precise(p_mat, bd)
        t_inv = p_mat

        gc0 = gc[:, 0:1]
        gc1 = gc[:, ck:ck + 1]
        gc128 = jnp.where(first128, gc0, gc1)
        gl128 = gc128[ck - 1:ck, :]
        beta128 = jnp.where(first128, beta0, beta1)
        e_g = jnp.exp(gc128)
        kf = k.astype(F32)
        k2 = jnp.concatenate([kf, kf], axis=1)
        qf = q.astype(F32)
        q2 = jnp.concatenate([qf, qf], axis=1)
        v2 = v_ref[rows, :].astype(F32)
        vb = v2 * beta128
        kbg = k2 * beta128 * e_g
        rhs = jnp.concatenate([_block_diag2(vb, hd), _block_diag2(kbg, hd)], axis=1).astype(BF)
        wk = dot(t_inv.astype(BF), rhs)
        w_vals = wk[:, :2 * hd]
        k_cum = wk[:, 2 * hd:]

        state = state_ref[...]
        lhs = jnp.concatenate([k_cum, q2 * e_g], axis=0).astype(BF)
        ks_qs = dot(lhs, state.astype(BF))
        v_new = w_vals - ks_qs[:ck]
        v_bd = _block_diag2(v_new, hd).astype(BF)
        o = ks_qs[ck:] + dot(attn.astype(BF), v_bd)
        k_state = _block_diag2(k2 * jnp.exp(gl128 - gc128), hd).astype(BF)
        upd = lax.dot_general(k_state, v_bd, (((0,), (0,)), ((), ())), preferred_element_type=F32)
        state_ref[...] = state * jnp.exp(gl128) + upd

        z = z_ref[rows, :].astype(F32)
        ng = ng_ref[...]
        for hh in range(2):
            sl = slice(hh * hd, (hh + 1) * hd)
            oh = o[:, sl]
            ms = jnp.mean(oh * oh, axis=-1, keepdims=True)
            zh = z[:, sl]
            o_ref[rows, sl] = (oh * lax.rsqrt(ms + EPS) * ng * (zh * jax.nn.sigmoid(zh))).astype(o_ref.dtype)
        return carry

    lax.fori_loop(0, ts // ck, chunk, 0)


def _gdn_delta(qkv, zsrc, z_col0, bg, norm_g, d_qk, d_v, ts=1024):
    b, s, _ = qkv.shape
    hd = GDN_HEAD_DIM
    n_heads = d_v // hd
    n_pairs = n_heads // 2
    assert d_qk // hd == n_pairs and z_col0 % (2 * hd) == 0
    k_blk0 = d_qk // hd
    v_blk0 = 2 * d_qk // (2 * hd)
    z_blk0 = z_col0 // (2 * hd)
    return pl.pallas_call(
        functools.partial(_gdn_delta_kernel, n_heads=n_heads),
        grid=(b, n_pairs, s // ts),
        in_specs=[pl.BlockSpec((None, ts, hd), lambda bi, pi, si: (bi, si, pi)),
                  pl.BlockSpec((None, ts, hd), lambda bi, pi, si: (bi, si, k_blk0 + pi)),
                  pl.BlockSpec((None, ts, 2 * hd), lambda bi, pi, si: (bi, si, v_blk0 + pi)),
                  pl.BlockSpec((None, ts, 2 * hd), lambda bi, pi, si: (bi, si, z_blk0 + pi)),
                  pl.BlockSpec((None, ts, LANES), lambda bi, pi, si: (bi, si, 0)),
                  pl.BlockSpec((1, hd), lambda bi, pi, si: (0, 0))],
        out_specs=pl.BlockSpec((None, ts, 2 * hd), lambda bi, pi, si: (bi, si, pi)),
        out_shape=jax.ShapeDtypeStruct((b, s, d_v), BF),
        scratch_shapes=[pltpu.VMEM((2 * hd, 2 * hd), F32)],
        compiler_params=_params(3),
        name="gdn_delta",
    )(qkv, qkv, qkv, zsrc, bg, norm_g.reshape(1, hd))


def kernel(x, mem, norm_mix_g, norm_mem_g, mem_kv_w, norm_ffn_g, ffn_w_in, ffn_w_out, lru_w_in, lru_w_out, lru_conv_w, lru_conv_b, lru_gate_a_w, lru_gate_a_b, lru_gate_x_w, lru_gate_x_b, lru_lambda, gdn_w_in, gdn_w_out, gdn_conv_w, gdn_a_log, gdn_dt_bias, gdn_norm_g, final_norm_g):
    b, s, d = x.shape
    m = b * s
    mem_len = mem.shape[1]
    depth = norm_mix_g.shape[0]
    d_rnn = lru_lambda.shape[1]
    n_vh = gdn_a_log.shape[1]
    d_v = n_vh * GDN_HEAD_DIM
    d_qk = (gdn_conv_w.shape[2] - d_v) // 2
    gdn_main = 2 * d_qk + 2 * d_v

    h = x.reshape(m, d)
    mem2 = mem.reshape(b * mem_len, d)
    for i in range(depth):
        j = i // 2
        hn = _rmsnorm(h, norm_mix_g[i], BF)
        mem_n = _rmsnorm(mem2, norm_mem_g[i], BF)
        kv = _matmul([mem_n], mem_kv_w[i], 0, 2 * d, BF).reshape(b, mem_len, 2 * d)
        if i % 2 == 0:
            w_in = lru_w_in[j]
            proj = _matmul([hn], w_in, 0, 2 * d_rnn, BF, tm=1024).reshape(b, s, 2 * d_rnn)
            xq = _matmul([hn], w_in, 2 * d_rnn, d, BF, tm=1024)
            y = _rglru(proj, lru_conv_w[j], lru_conv_b[j], lru_gate_a_w[j], lru_gate_a_b[j],
                       lru_gate_x_w[j], lru_gate_x_b[j], lru_lambda[j], d_rnn)
            w_out = lru_w_out[j]
        else:
            w_in = gdn_w_in[j]
            proj = _matmul([hn], w_in, 0, gdn_main, BF, tm=1024).reshape(b, s, gdn_main)
            w_ba = jnp.pad(w_in[:, gdn_main:gdn_main + 2 * n_vh], ((0, 0), (0, LANES - 2 * n_vh)))
            ba = _matmul([hn], w_ba, 0, LANES, F32, tm=1024).reshape(b, s, LANES)
            xq = _matmul([hn], w_in[:, gdn_main + 2 * n_vh:], 0, d, BF, tm=1024)
            qkv, bg = _gdn_prep(proj, ba, gdn_conv_w[j], gdn_a_log[j], gdn_dt_bias[j], d_qk, d_v)
            y = _gdn_delta(qkv, proj, 2 * d_qk + d_v, bg, gdn_norm_g[j], d_qk, d_v)
            w_out = gdn_w_out[j]
        xo = _xattn(xq.reshape(b, s, d), kv)
        y2 = y.reshape(m, y.shape[-1])
        h = _matmul([y2, xo.reshape(m, d)], w_out, 0, d, F32, residual=h)
        hn = _rmsnorm(h, norm_ffn_g[i], BF)
        act = _ffn_in(hn, ffn_w_in[i])
        h = _matmul([act], ffn_w_out[i], 0, d, F32, residual=h)
    return _rmsnorm(h, final_norm_g, F32).reshape(b, s, d)
```

```python
import functools

import jax
import jax.numpy as jnp
from jax import lax
from jax.experimental import pallas as pl
from jax.experimental.pallas import tpu as pltpu

F32 = jnp.float32
BF = jnp.bfloat16

EPS = 1e-6
CONV_W = 4
LRU_BLOCKS = 16
LRU_C = 8.0
GDN_HEAD_DIM = 128
GDN_CHUNK = 64
X_HEADS = 4

LANES = 128
SUBLANES = 8
VMEM_LIMIT = 56 * 1024 * 1024


def _params(n_axes):
    return pltpu.CompilerParams(
        dimension_semantics=("arbitrary",) * n_axes, vmem_limit_bytes=VMEM_LIMIT)


def _rmsnorm_kernel(x_ref, g_ref, o_ref):
    x = x_ref[...]
    ms = jnp.mean(x * x, axis=-1, keepdims=True)
    o_ref[...] = (x * lax.rsqrt(ms + EPS) * g_ref[...]).astype(o_ref.dtype)


def _rmsnorm(x, g, out_dtype, tm=512):
    m, d = x.shape
    tm = min(tm, m)
    return pl.pallas_call(
        _rmsnorm_kernel,
        grid=(m // tm,),
        in_specs=[pl.BlockSpec((tm, d), lambda i: (i, 0)),
                  pl.BlockSpec((1, d), lambda i: (0, 0))],
        out_specs=pl.BlockSpec((tm, d), lambda i: (i, 0)),
        out_shape=jax.ShapeDtypeStruct((m, d), out_dtype),
        compiler_params=_params(1),
        name="rmsnorm",
    )(x, g.reshape(1, d))


_CAST_ROWS = 512


def _cast_weight(w_ref, wb_ref):
    k = w_ref.shape[0]
    rows = _CAST_ROWS if k % _CAST_ROWS == 0 else k

    def body(r, carry):
        r0 = pl.multiple_of(r * rows, rows)
        wb_ref[pl.ds(r0, rows), :] = w_ref[pl.ds(r0, rows), :].astype(BF)
        return carry

    lax.fori_loop(0, k // rows, body, 0)


def _mm_kernel(*refs, k_sizes, has_res):
    n_x = len(k_sizes)
    x_refs = refs[:n_x]
    w_ref = refs[n_x]
    res_ref = refs[n_x + 1] if has_res else None
    o_ref = refs[n_x + 1 + has_res]
    wb_ref = refs[n_x + 2 + has_res]

    @pl.when(pl.program_id(1) == 0)
    def _():
        _cast_weight(w_ref, wb_ref)

    acc = None
    off = 0
    for x_ref, k in zip(x_refs, k_sizes):
        part = jnp.dot(x_ref[...], wb_ref[off:off + k, :], preferred_element_type=F32)
        acc = part if acc is None else acc + part
        off += k
    if has_res:
        acc = acc + res_ref[...]
    o_ref[...] = acc.astype(o_ref.dtype)


def _matmul(xs, w, layer, col0, n_cols, out_dtype, residual=None, tm=512, tn=512):
    m = xs[0].shape[0]
    k_sizes = tuple(x.shape[1] for x in xs)
    k_tot = sum(k_sizes)
    assert w.shape[1] == k_tot
    tm = min(tm, m)
    tn = min(tn, n_cols)
    assert m % tm == 0 and n_cols % tn == 0 and col0 % tn == 0
    cb0 = col0 // tn
    in_specs = [pl.BlockSpec((tm, k), lambda j, i: (i, 0)) for k in k_sizes]
    in_specs.append(pl.BlockSpec((None, k_tot, tn), lambda j, i: (layer, 0, j + cb0)))
    args = list(xs) + [w]
    if residual is not None:
        in_specs.append(pl.BlockSpec((tm, tn), lambda j, i: (i, j)))
        args.append(residual)
    return pl.pallas_call(
        functools.partial(_mm_kernel, k_sizes=k_sizes, has_res=residual is not None),
        grid=(n_cols // tn, m // tm),
        in_specs=in_specs,
        out_specs=pl.BlockSpec((tm, tn), lambda j, i: (i, j)),
        out_shape=jax.ShapeDtypeStruct((m, n_cols), out_dtype),
        scratch_shapes=[pltpu.VMEM((k_tot, tn), BF)],
        compiler_params=_params(2),
        name="matmul",
    )(*args)


def _ffn_in_kernel(x_ref, wg_ref, wu_ref, o_ref, wgb_ref, wub_ref):
    @pl.when(pl.program_id(1) == 0)
    def _():
        _cast_weight(wg_ref, wgb_ref)
        _cast_weight(wu_ref, wub_ref)

    x = x_ref[...]
    gate = jnp.dot(x, wgb_ref[...], preferred_element_type=F32)
    up = jnp.dot(x, wub_ref[...], preferred_element_type=F32)
    o_ref[...] = (gate * jax.nn.sigmoid(gate) * up).astype(o_ref.dtype)


def _ffn_in(x, w, layer, tm=1024, tn=512):
    m, k = x.shape
    f = w.shape[2] // 2
    assert m % tm == 0 and f % tn == 0
    nb = f // tn
    return pl.pallas_call(
        _ffn_in_kernel,
        grid=(nb, m // tm),
        in_specs=[pl.BlockSpec((tm, k), lambda j, i: (i, 0)),
                  pl.BlockSpec((None, k, tn), lambda j, i: (layer, 0, j)),
                  pl.BlockSpec((None, k, tn), lambda j, i: (layer, 0, j + nb))],
        out_specs=pl.BlockSpec((tm, tn), lambda j, i: (i, j)),
        out_shape=jax.ShapeDtypeStruct((m, f), BF),
        scratch_shapes=[pltpu.VMEM((k, tn), BF), pltpu.VMEM((k, tn), BF)],
        compiler_params=_params(2),
        name="ffn_in",
    )(x, w, w)


def _xattn_kernel(q_ref, k_ref, v_ref, o_ref, *, heads, hd):
    scale = hd ** -0.5
    for h in range(heads):
        sl = slice(h * hd, (h + 1) * hd)
        s = lax.dot_general(q_ref[:, sl], k_ref[:, sl], (((1,), (1,)), ((), ())),
                            preferred_element_type=F32) * scale
        mx = jnp.max(s, axis=-1, keepdims=True)
        p = jnp.exp(s - mx)
        denom = jnp.sum(p, axis=-1, keepdims=True)
        o = jnp.dot(p.astype(BF), v_ref[:, sl], preferred_element_type=F32)
        o_ref[:, sl] = (o / denom).astype(o_ref.dtype)


def _xattn(q, kv, ts=512):
    b, s, d = q.shape
    mem = kv.shape[1]
    return pl.pallas_call(
        functools.partial(_xattn_kernel, heads=X_HEADS, hd=d // X_HEADS),
        grid=(b, s // ts),
        in_specs=[pl.BlockSpec((None, ts, d), lambda bi, si: (bi, si, 0)),
                  pl.BlockSpec((None, mem, d), lambda bi, si: (bi, 0, 0)),
                  pl.BlockSpec((None, mem, d), lambda bi, si: (bi, 0, 1))],
        out_specs=pl.BlockSpec((None, ts, d), lambda bi, si: (bi, si, 0)),
        out_shape=jax.ShapeDtypeStruct((b, s, d), BF),
        compiler_params=_params(2),
        name="xattn",
    )(q, kv, kv)


def _causal_conv(x, tail, cw_ref, cols):
    t = x.shape[0]
    row8 = lax.broadcasted_iota(jnp.int32, (SUBLANES, x.shape[1]), 0)
    acc = x * cw_ref[CONV_W - 1:CONV_W, cols]
    for d in range(1, CONV_W):
        xr = pltpu.roll(x, d, axis=0)
        head = jnp.where(row8 < d, pltpu.roll(tail, d, axis=0), xr[0:SUBLANES])
        xs = jnp.concatenate([head, xr[SUBLANES:t]], axis=0)
        acc = acc + xs * cw_ref[CONV_W - 1 - d:CONV_W - d, cols]
    return acc


def _band_layout(c, tn, blk):
    starts, need = [], 0
    for j in range(c // tn):
        lo = (j * tn // blk) * blk
        hi = (((j + 1) * tn - 1) // blk + 1) * blk
        st = (lo // LANES) * LANES
        starts.append(st)
        need = max(need, hi - st)
    kw = -(-need // LANES) * LANES
    starts = [min(st, c - kw) for st in starts]
    return tuple(starts), kw


def _band_weights(gw, starts, kw, tn):
    dense = jax.scipy.linalg.block_diag(*[gw[i] for i in range(gw.shape[0])])
    return jnp.stack([dense[st:st + kw, j * tn:(j + 1) * tn]
                      for j, st in enumerate(starts)]).astype(BF)


def _scan_linear(a, b):
    t = a.shape[0]
    row = lax.broadcasted_iota(jnp.int32, a.shape, 0)
    d = 1
    while d < SUBLANES:
        keep = row >= d
        a_s = jnp.where(keep, pltpu.roll(a, d, axis=0), 1.0)
        b_s = jnp.where(keep, pltpu.roll(b, d, axis=0), 0.0)
        b = a * b_s + b
        a = a * a_s
        d *= 2
    while d < t:
        b = jnp.concatenate([b[:d], a[d:] * b[:t - d] + b[d:]], axis=0)
        a = jnp.concatenate([a[:d], a[d:] * a[:t - d]], axis=0)
        d *= 2
    return a, b


def _rglru_kernel(xb_ref, gb_ref, cw_ref, cb_ref, wa_ref, wx_ref, ba_ref, bx_ref, lam_ref,
                  y_ref, tail_ref, hc_ref, xc_ref, xcb_ref, *, starts, kw, tn):
    t, c = xb_ref.shape

    @pl.when(pl.program_id(1) == 0)
    def _():
        tail_ref[...] = jnp.zeros_like(tail_ref)
        hc_ref[...] = jnp.zeros_like(hc_ref)

    for j in range(c // tn):
        cols = slice(j * tn, (j + 1) * tn)
        x = xb_ref[:, cols].astype(F32)
        xc = _causal_conv(x, tail_ref[:, cols], cw_ref, cols) + cb_ref[:, cols]
        tail_ref[:, cols] = x[t - SUBLANES:t]
        xc_ref[:, cols] = xc
        xcb_ref[:, cols] = xc.astype(BF)

    for j, st in enumerate(starts):
        cols = slice(j * tn, (j + 1) * tn)
        xw = xcb_ref[:, st:st + kw]
        r = jax.nn.sigmoid(jnp.dot(xw, wa_ref[j], preferred_element_type=F32) + ba_ref[:, cols])
        i = jax.nn.sigmoid(jnp.dot(xw, wx_ref[j], preferred_element_type=F32) + bx_ref[:, cols])
        neg_lam = -lam_ref[:, cols]
        softplus = jnp.maximum(neg_lam, 0.0) + jnp.log1p(jnp.exp(-jnp.abs(neg_lam)))
        log_a = (-LRU_C) * r * softplus
        a = jnp.exp(log_a)
        th = jnp.tanh(log_a)
        b_in = jnp.sqrt(-2.0 * th / (1.0 - th)) * (i * xc_ref[:, cols])
        a_cum, h_loc = _scan_linear(a, b_in)
        h = h_loc + a_cum * hc_ref[0:1, cols]
        hc_ref[:, cols] = jnp.broadcast_to(h[t - 1:t], (SUBLANES, tn))
        y_ref[:, cols] = (h * jax.nn.gelu(gb_ref[:, cols].astype(F32))).astype(y_ref.dtype)


def _rglru(proj, conv_w, conv_b, gate_a_w, gate_a_b, gate_x_w, gate_x_b, lam, c, t=128, tn=256):
    b, s, _ = proj.shape
    starts, kw = _band_layout(c, tn, c // LRU_BLOCKS)
    wa = _band_weights(gate_a_w, starts, kw, tn)
    wx = _band_weights(gate_x_w, starts, kw, tn)
    nt = c // tn
    row = lambda v: v.reshape(1, c)
    full = lambda shape: pl.BlockSpec(shape, lambda bi, ti: (0,) * len(shape))
    return pl.pallas_call(
        functools.partial(_rglru_kernel, starts=starts, kw=kw, tn=tn),
        grid=(b, s // t),
        in_specs=[pl.BlockSpec((None, t, c), lambda bi, ti: (bi, ti, 0)),
                  pl.BlockSpec((None, t, c), lambda bi, ti: (bi, ti, 1)),
                  full((CONV_W, c)), full((1, c)),
                  full((nt, kw, tn)), full((nt, kw, tn)),
                  full((1, c)), full((1, c)), full((1, c))],
        out_specs=pl.BlockSpec((None, t, c), lambda bi, ti: (bi, ti, 0)),
        out_shape=jax.ShapeDtypeStruct((b, s, c), BF),
        scratch_shapes=[pltpu.VMEM((SUBLANES, c), F32), pltpu.VMEM((SUBLANES, c), F32),
                        pltpu.VMEM((t, c), F32), pltpu.VMEM((t, c), BF)],
        compiler_params=_params(2),
        name="rglru",
    )(proj, proj, conv_w, row(conv_b), wa, wx, row(gate_a_b), row(gate_x_b), row(lam))


def _gdn_prep_kernel(x_ref, ba_ref, cw_ref, alog_ref, dtb_ref, o_ref, bg_ref, tail_ref,
                     *, d_qk, n_heads, tc):
    t, c = x_ref.shape

    @pl.when(pl.program_id(1) == 0)
    def _():
        tail_ref[...] = jnp.zeros_like(tail_ref)

    hd = GDN_HEAD_DIM
    for j in range(c // tc):
        cols = slice(j * tc, (j + 1) * tc)
        x = x_ref[:, cols].astype(F32)
        y = _causal_conv(x, tail_ref[:, cols], cw_ref, cols)
        tail_ref[:, cols] = x[t - SUBLANES:t]
        y = y * jax.nn.sigmoid(y)
        if j * tc < 2 * d_qk:
            scale = hd ** -0.5 if j * tc < d_qk else 1.0
            for hh in range(tc // hd):
                yh = y[:, hh * hd:(hh + 1) * hd]
                ss = jnp.sum(yh * yh, axis=-1, keepdims=True)
                yn = yh * lax.rsqrt(ss + EPS)
                if scale != 1.0:
                    yn = yn * scale
                o_ref[:, j * tc + hh * hd:j * tc + (hh + 1) * hd] = yn.astype(o_ref.dtype)
        else:
            o_ref[:, cols] = y.astype(o_ref.dtype)

    ba = ba_ref[...]
    lane = lax.broadcasted_iota(jnp.int32, ba.shape, 1)
    z = ba + dtb_ref[...]
    softplus = jnp.maximum(z, 0.0) + jnp.log1p(jnp.exp(-jnp.abs(z)))
    g = -jnp.exp(alog_ref[...]) * softplus
    bg_ref[...] = jnp.where(lane < n_heads, jax.nn.sigmoid(ba), g)


def _gdn_prep(proj, ba, conv_w, a_log, dt_bias, d_qk, d_v, t=256, tc=512):
    b, s, _ = proj.shape
    c = 2 * d_qk + d_v
    nh = a_log.shape[0]
    pad = lambda v: jnp.zeros((1, LANES), F32).at[0, nh:2 * nh].set(v)
    full = lambda shape: pl.BlockSpec(shape, lambda bi, ti: (0,) * len(shape))
    return pl.pallas_call(
        functools.partial(_gdn_prep_kernel, d_qk=d_qk, n_heads=nh, tc=tc),
        grid=(b, s // t),
        in_specs=[pl.BlockSpec((None, t, c), lambda bi, ti: (bi, ti, 0)),
                  pl.BlockSpec((None, t, LANES), lambda bi, ti: (bi, ti, 0)),
                  full((CONV_W, c)), full((1, LANES)), full((1, LANES))],
        out_specs=[pl.BlockSpec((None, t, c), lambda bi, ti: (bi, ti, 0)),
                   pl.BlockSpec((None, t, LANES), lambda bi, ti: (bi, ti, 0))],
        out_shape=[jax.ShapeDtypeStruct((b, s, c), BF),
                   jax.ShapeDtypeStruct((b, s, LANES), F32)],
        scratch_shapes=[pltpu.VMEM((SUBLANES, c), F32)],
        compiler_params=_params(2),
        name="gdn_prep",
    )(proj, ba, conv_w, pad(a_log), pad(dt_bias))


def _split2(x):
    hi = x.astype(BF)
    mid = (x - hi.astype(F32)).astype(BF)
    return hi, mid


def _dot_precise(a, b):
    a_hi, a_mid = _split2(a)
    b_hi, b_mid = _split2(b)
    lhs = jnp.concatenate([a_hi, a_mid, a_hi], axis=1)
    rhs = jnp.concatenate([b_hi, b_hi, b_mid], axis=0)
    return jnp.dot(lhs, rhs, preferred_element_type=F32)


def _block_diag2(x, half):
    lane = lax.broadcasted_iota(jnp.int32, x.shape, 1)
    zero = jnp.zeros_like(x)
    return jnp.concatenate([jnp.where(lane < half, x, zero), jnp.where(lane >= half, x, zero)], axis=0)


def _gdn_delta_kernel(q_ref, k_ref, v_ref, z_ref, bg_ref, ng_ref, o_ref,
                      state_ref, w_ref, lhs_ref, attn_ref, ks_ref, dec_ref, *, n_heads, pp):
    ck = GDN_CHUNK
    hd = GDN_HEAD_DIM
    ts = q_ref.shape[0]
    n_chunks = ts // ck
    pair0 = pl.program_id(1) * pp

    @pl.when(pl.program_id(2) == 0)
    def _():
        state_ref[...] = jnp.zeros_like(state_ref)

    lane64 = lax.broadcasted_iota(jnp.int32, (ck, 2 * ck), 1)
    row64 = lax.broadcasted_iota(jnp.int32, (ck, 2 * ck), 0)
    col_in_head = jnp.where(lane64 < ck, lane64, lane64 - ck)
    first64 = lane64 < ck
    causal = row64 >= col_in_head
    strict = row64 > col_in_head
    eye = row64 == col_in_head
    lane128 = lax.broadcasted_iota(jnp.int32, (ck, 2 * hd), 1)
    first128 = lane128 < hd
    lane_bg = lax.broadcasted_iota(jnp.int32, (ck, LANES), 1)
    dot = functools.partial(jnp.dot, preferred_element_type=F32)

    pairs = range(pp)
    n_sq = 1
    while (2 << n_sq) < ck:
        n_sq += 1

    def prepare(ci, carry):
        rows = pl.ds(pl.multiple_of(ci * ck, ck), ck)
        rows2 = pl.ds(pl.multiple_of(ci * 2 * ck, 2 * ck), 2 * ck)
        rows8 = pl.ds(pl.multiple_of(ci * SUBLANES, SUBLANES), SUBLANES)
        bg = bg_ref[rows, :]

        def column(l):
            return jnp.sum(jnp.where(lane_bg == l, bg, 0.0), axis=-1, keepdims=True)

        betas, gcs, qs, ks, a_mats = [], [], [], [], []
        for pi in pairs:
            h0 = 2 * (pair0 + pi)
            beta0, beta1 = column(h0), column(h0 + 1)
            gc = jnp.where(first64, column(n_heads + h0), column(n_heads + h0 + 1))
            d = 1
            while d < ck:
                if d < SUBLANES:
                    gc = gc + jnp.where(row64 >= d, pltpu.roll(gc, d, axis=0), 0.0)
                else:
                    gc = jnp.concatenate([gc[:d], gc[d:] + gc[:ck - d]], axis=0)
                d *= 2
            g_row = jnp.sum(jnp.where(eye, gc, 0.0), axis=0, keepdims=True)
            decay = jnp.where(causal, jnp.exp(jnp.where(causal, gc - g_row, 0.0)), 0.0)

            q = q_ref[rows, pi * hd:(pi + 1) * hd]
            k = k_ref[rows, pi * hd:(pi + 1) * hd]
            qk = jnp.concatenate([q, k], axis=0)
            kk = jnp.concatenate([k, k], axis=0)
            prod = lax.dot_general(qk, kk, (((1,), (1,)), ((), ())), preferred_element_type=F32)
            beta64 = jnp.where(first64, beta0, beta1)
            a_mats.append(jnp.where(strict, prod[ck:] * beta64 * decay, 0.0))
            attn_ref[pi, rows, :] = jnp.where(causal, prod[:ck] * decay, 0.0).astype(BF)
            betas.append((beta0, beta1))
            gcs.append(gc)
            qs.append(q)
            ks.append(k)

        p_mats = [jnp.where(eye, 1.0, 0.0) - a for a in a_mats]
        m_mats = [_dot_precise(a, _block_diag2(a, ck)) for a in a_mats]
        for step in range(n_sq):
            bds = [_block_diag2(m, ck) for m in m_mats]
            if step + 1 < n_sq:
                both = [_dot_precise(jnp.concatenate([p, m], axis=0), bd)
                        for p, m, bd in zip(p_mats, m_mats, bds)]
                p_mats = [p + bo[:ck] for p, bo in zip(p_mats, both)]
                m_mats = [bo[ck:] for bo in both]
            else:
                p_mats = [p + _dot_precise(p, bd) for p, bd in zip(p_mats, bds)]

        for pi in pairs:
            t_inv = p_mats[pi]
            gc = gcs[pi]
            gc128 = jnp.where(first128, gc[:, 0:1], gc[:, ck:ck + 1])
            gl128 = gc128[ck - 1:ck, :]
            beta128 = jnp.where(first128, betas[pi][0], betas[pi][1])
            e_g = jnp.exp(gc128)
            kf = ks[pi].astype(F32)
            k2 = jnp.concatenate([kf, kf], axis=1)
            qf = qs[pi].astype(F32)
            q2 = jnp.concatenate([qf, qf], axis=1)
            vb = v_ref[rows, pi * 2 * hd:(pi + 1) * 2 * hd].astype(F32) * beta128
            kbg = k2 * beta128 * e_g
            rhs = jnp.concatenate([_block_diag2(vb, hd), _block_diag2(kbg, hd)], axis=1).astype(BF)
            wk = dot(t_inv.astype(BF), rhs)
            w_ref[pi, rows, :] = wk[:, :2 * hd]
            lhs_ref[pi, rows2, :] = jnp.concatenate([wk[:, 2 * hd:], q2 * e_g], axis=0).astype(BF)
            ks_ref[pi, rows2, :] = _block_diag2(k2 * jnp.exp(gl128 - gc128), hd).astype(BF)
            dec_ref[pi, rows8, :] = jnp.broadcast_to(jnp.exp(gl128), (SUBLANES, 2 * hd))
        return carry

    lax.fori_loop(0, n_chunks, prepare, 0)

    def recur(ci, carry):
        rows = pl.ds(pl.multiple_of(ci * ck, ck), ck)
        rows2 = pl.ds(pl.multiple_of(ci * 2 * ck, 2 * ck), 2 * ck)
        row1 = pl.ds(pl.multiple_of(ci * SUBLANES, SUBLANES), 1)
        ng = ng_ref[...]
        states = [state_ref[pi] for pi in pairs]
        ks_qs = [dot(lhs_ref[pi, rows2, :], states[pi].astype(BF)) for pi in pairs]
        v_bds = [_block_diag2(w_ref[pi, rows, :] - ks_qs[pi][:ck], hd).astype(BF) for pi in pairs]
        upds = [lax.dot_general(ks_ref[pi, rows2, :], v_bds[pi], (((0,), (0,)), ((), ())),
                                preferred_element_type=F32) for pi in pairs]
        for pi in pairs:
            state_ref[pi] = states[pi] * dec_ref[pi, row1, :] + upds[pi]
        outs = [ks_qs[pi][ck:] + dot(attn_ref[pi, rows, :], v_bds[pi]) for pi in pairs]

        for pi in pairs:
            for hh in range(2):
                sl = slice((2 * pi + hh) * hd, (2 * pi + hh + 1) * hd)
                oh = outs[pi][:, hh * hd:(hh + 1) * hd]
                ms = jnp.mean(oh * oh, axis=-1, keepdims=True)
                zh = z_ref[rows, sl].astype(F32)
                o_ref[rows, sl] = (oh * lax.rsqrt(ms + EPS) * ng * (zh * jax.nn.sigmoid(zh))).astype(o_ref.dtype)
        return carry

    lax.fori_loop(0, n_chunks, recur, 0)


def _gdn_delta(qkv, zsrc, z_col0, bg, norm_g, d_qk, d_v, ts=1024, pp=4):
    b, s, _ = qkv.shape
    hd = GDN_HEAD_DIM
    ck = GDN_CHUNK
    n_heads = d_v // hd
    n_pairs = n_heads // 2
    ts = min(ts, s)
    assert d_qk // hd == n_pairs and n_pairs % pp == 0 and s % ts == 0
    qw = pp * hd
    vw = 2 * pp * hd
    assert d_qk % qw == 0 and (2 * d_qk) % vw == 0 and z_col0 % vw == 0
    k_blk0 = d_qk // qw
    v_blk0 = 2 * d_qk // vw
    z_blk0 = z_col0 // vw
    return pl.pallas_call(
        functools.partial(_gdn_delta_kernel, n_heads=n_heads, pp=pp),
        grid=(b, n_pairs // pp, s // ts),
        in_specs=[pl.BlockSpec((None, ts, qw), lambda bi, pi, si: (bi, si, pi)),
                  pl.BlockSpec((None, ts, qw), lambda bi, pi, si: (bi, si, k_blk0 + pi)),
                  pl.BlockSpec((None, ts, vw), lambda bi, pi, si: (bi, si, v_blk0 + pi)),
                  pl.BlockSpec((None, ts, vw), lambda bi, pi, si: (bi, si, z_blk0 + pi)),
                  pl.BlockSpec((None, ts, LANES), lambda bi, pi, si: (bi, si, 0)),
                  pl.BlockSpec((1, hd), lambda bi, pi, si: (0, 0))],
        out_specs=pl.BlockSpec((None, ts, vw), lambda bi, pi, si: (bi, si, pi)),
        out_shape=jax.ShapeDtypeStruct((b, s, d_v), BF),
        scratch_shapes=[pltpu.VMEM((pp, 2 * hd, 2 * hd), F32),
                        pltpu.VMEM((pp, ts, 2 * hd), F32),
                        pltpu.VMEM((pp, 2 * ts, 2 * hd), BF),
                        pltpu.VMEM((pp, ts, 2 * ck), BF),
                        pltpu.VMEM((pp, 2 * ts, 2 * hd), BF),
                        pltpu.VMEM((pp, ts // ck * SUBLANES, 2 * hd), F32)],
        compiler_params=_params(3),
        name="gdn_delta",
    )(qkv, qkv, qkv, zsrc, bg, norm_g.reshape(1, hd))


def kernel(x, mem, norm_mix_g, norm_mem_g, mem_kv_w, norm_ffn_g, ffn_w_in, ffn_w_out, lru_w_in, lru_w_out, lru_conv_w, lru_conv_b, lru_gate_a_w, lru_gate_a_b, lru_gate_x_w, lru_gate_x_b, lru_lambda, gdn_w_in, gdn_w_out, gdn_conv_w, gdn_a_log, gdn_dt_bias, gdn_norm_g, final_norm_g):
    b, s, d = x.shape
    m = b * s
    mem_len = mem.shape[1]
    depth = norm_mix_g.shape[0]
    d_rnn = lru_lambda.shape[1]
    n_vh = gdn_a_log.shape[1]
    d_v = n_vh * GDN_HEAD_DIM
    d_qk = (gdn_conv_w.shape[2] - d_v) // 2
    gdn_main = 2 * d_qk + 2 * d_v

    h = x.reshape(m, d)
    mem2 = mem.reshape(b * mem_len, d)
    for i in range(depth):
        j = i // 2
        hn = _rmsnorm(h, norm_mix_g[i], BF)
        mem_n = _rmsnorm(mem2, norm_mem_g[i], BF)
        kv = _matmul([mem_n], mem_kv_w, i, 0, 2 * d, BF).reshape(b, mem_len, 2 * d)
        if i % 2 == 0:
            proj = _matmul([hn], lru_w_in, j, 0, 2 * d_rnn, BF, tm=1024).reshape(b, s, 2 * d_rnn)
            xq = _matmul([hn], lru_w_in, j, 2 * d_rnn, d, BF, tm=1024)
            y = _rglru(proj, lru_conv_w[j], lru_conv_b[j], lru_gate_a_w[j], lru_gate_a_b[j],
                       lru_gate_x_w[j], lru_gate_x_b[j], lru_lambda[j], d_rnn)
            w_out = lru_w_out
        else:
            proj = _matmul([hn], gdn_w_in, j, 0, gdn_main, BF, tm=1024).reshape(b, s, gdn_main)
            w_tail = gdn_w_in[j, :, gdn_main:]
            w_ba = jnp.pad(w_tail[:, :2 * n_vh], ((0, 0), (0, LANES - 2 * n_vh)))[None]
            ba = _matmul([hn], w_ba, 0, 0, LANES, F32, tm=1024).reshape(b, s, LANES)
            xq = _matmul([hn], w_tail[None, :, 2 * n_vh:], 0, 0, d, BF, tm=1024)
            qkv, bg = _gdn_prep(proj, ba, gdn_conv_w[j], gdn_a_log[j], gdn_dt_bias[j], d_qk, d_v)
            y = _gdn_delta(qkv, proj, 2 * d_qk + d_v, bg, gdn_norm_g[j], d_qk, d_v)
            w_out = gdn_w_out
        xo = _xattn(xq.reshape(b, s, d), kv)
        y2 = y.reshape(m, y.shape[-1])
        h = _matmul([y2, xo.reshape(m, d)], w_out, j, 0, d, F32, residual=h)
        hn = _rmsnorm(h, norm_ffn_g[i], BF)
        act = _ffn_in(hn, ffn_w_in, i)
        h = _matmul([act], ffn_w_out, i, 0, d, F32, residual=h)
    return _rmsnorm(h, final_norm_g, F32).reshape(b, s, d)
```

```python
import functools

import jax
import jax.numpy as jnp
from jax import lax
from jax.experimental import pallas as pl
from jax.experimental.pallas import tpu as pltpu

F32 = jnp.float32
BF = jnp.bfloat16

EPS = 1e-6
CONV_W = 4
LRU_BLOCKS = 16
LRU_C = 8.0
GDN_HEAD_DIM = 128
GDN_CHUNK = 64
X_HEADS = 4

LANES = 128
SUBLANES = 8
VMEM_LIMIT = 56 * 1024 * 1024


def _params(n_axes):
    return pltpu.CompilerParams(
        dimension_semantics=("arbitrary",) * n_axes, vmem_limit_bytes=VMEM_LIMIT)


def _rmsnorm_kernel(x_ref, g_ref, o_ref):
    x = x_ref[...]
    ms = jnp.mean(x * x, axis=-1, keepdims=True)
    o_ref[...] = (x * lax.rsqrt(ms + EPS) * g_ref[...]).astype(o_ref.dtype)


def _rmsnorm(x, g, out_dtype, tm=512):
    m, d = x.shape
    tm = min(tm, m)
    return pl.pallas_call(
        _rmsnorm_kernel,
        grid=(m // tm,),
        in_specs=[pl.BlockSpec((tm, d), lambda i: (i, 0)),
                  pl.BlockSpec((1, d), lambda i: (0, 0))],
        out_specs=pl.BlockSpec((tm, d), lambda i: (i, 0)),
        out_shape=jax.ShapeDtypeStruct((m, d), out_dtype),
        compiler_params=_params(1),
        name="rmsnorm",
    )(x, g.reshape(1, d))


_CAST_ROWS = 512


def _cast_weight(w_ref, wb_ref):
    k = w_ref.shape[0]
    rows = _CAST_ROWS if k % _CAST_ROWS == 0 else k

    def body(r, carry):
        r0 = pl.multiple_of(r * rows, rows)
        wb_ref[pl.ds(r0, rows), :] = w_ref[pl.ds(r0, rows), :].astype(BF)
        return carry

    lax.fori_loop(0, k // rows, body, 0)


def _mm_kernel(*refs, k_sizes, has_res):
    n_x = len(k_sizes)
    x_refs = refs[:n_x]
    w_ref = refs[n_x]
    res_ref = refs[n_x + 1] if has_res else None
    o_ref = refs[n_x + 1 + has_res]
    wb_ref = refs[n_x + 2 + has_res]

    @pl.when(pl.program_id(1) == 0)
    def _():
        _cast_weight(w_ref, wb_ref)

    acc = None
    off = 0
    for x_ref, k in zip(x_refs, k_sizes):
        part = jnp.dot(x_ref[...], wb_ref[off:off + k, :], preferred_element_type=F32)
        acc = part if acc is None else acc + part
        off += k
    if has_res:
        acc = acc + res_ref[...]
    o_ref[...] = acc.astype(o_ref.dtype)


def _matmul(xs, w, layer, col0, n_cols, out_dtype, residual=None, tm=512, tn=512):
    m = xs[0].shape[0]
    k_sizes = tuple(x.shape[1] for x in xs)
    k_tot = sum(k_sizes)
    assert w.shape[1] == k_tot
    tm = min(tm, m)
    tn = min(tn, n_cols)
    assert m % tm == 0 and n_cols % tn == 0 and col0 % tn == 0
    cb0 = col0 // tn
    in_specs = [pl.BlockSpec((tm, k), lambda j, i: (i, 0)) for k in k_sizes]
    in_specs.append(pl.BlockSpec((None, k_tot, tn), lambda j, i: (layer, 0, j + cb0)))
    args = list(xs) + [w]
    if residual is not None:
        in_specs.append(pl.BlockSpec((tm, tn), lambda j, i: (i, j)))
        args.append(residual)
    return pl.pallas_call(
        functools.partial(_mm_kernel, k_sizes=k_sizes, has_res=residual is not None),
        grid=(n_cols // tn, m // tm),
        in_specs=in_specs,
        out_specs=pl.BlockSpec((tm, tn), lambda j, i: (i, j)),
        out_shape=jax.ShapeDtypeStruct((m, n_cols), out_dtype),
        scratch_shapes=[pltpu.VMEM((k_tot, tn), BF)],
        compiler_params=_params(2),
        name="matmul",
    )(*args)


def _copy_kernel(w_ref, o_ref):
    o_ref[...] = w_ref[...]


def _column_window(w, layer, col0, n_blocks):
    _, k, _ = w.shape
    assert col0 % LANES == 0
    cb0 = col0 // LANES
    return pl.pallas_call(
        _copy_kernel,
        grid=(n_blocks,),
        in_specs=[pl.BlockSpec((None, k, LANES), lambda j: (layer, 0, j + cb0))],
        out_specs=pl.BlockSpec((k, LANES), lambda j: (0, j)),
        out_shape=jax.ShapeDtypeStruct((k, n_blocks * LANES), w.dtype),
        compiler_params=_params(1),
        name="column_window",
    )(w)


def _ffn_in_kernel(x_ref, wg_ref, wu_ref, o_ref, wgb_ref, wub_ref):
    @pl.when(pl.program_id(1) == 0)
    def _():
        _cast_weight(wg_ref, wgb_ref)
        _cast_weight(wu_ref, wub_ref)

    x = x_ref[...]
    gate = jnp.dot(x, wgb_ref[...], preferred_element_type=F32)
    up = jnp.dot(x, wub_ref[...], preferred_element_type=F32)
    o_ref[...] = (gate * jax.nn.sigmoid(gate) * up).astype(o_ref.dtype)


def _ffn_in(x, w, layer, tm=1024, tn=512):
    m, k = x.shape
    f = w.shape[2] // 2
    assert m % tm == 0 and f % tn == 0
    nb = f // tn
    return pl.pallas_call(
        _ffn_in_kernel,
        grid=(nb, m // tm),
        in_specs=[pl.BlockSpec((tm, k), lambda j, i: (i, 0)),
                  pl.BlockSpec((None, k, tn), lambda j, i: (layer, 0, j)),
                  pl.BlockSpec((None, k, tn), lambda j, i: (layer, 0, j + nb))],
        out_specs=pl.BlockSpec((tm, tn), lambda j, i: (i, j)),
        out_shape=jax.ShapeDtypeStruct((m, f), BF),
        scratch_shapes=[pltpu.VMEM((k, tn), BF), pltpu.VMEM((k, tn), BF)],
        compiler_params=_params(2),
        name="ffn_in",
    )(x, w, w)


def _xattn_kernel(q_ref, k_ref, v_ref, o_ref, *, heads, hd):
    scale = hd ** -0.5
    for h in range(heads):
        sl = slice(h * hd, (h + 1) * hd)
        s = lax.dot_general(q_ref[:, sl], k_ref[:, sl], (((1,), (1,)), ((), ())),
                            preferred_element_type=F32) * scale
        mx = jnp.max(s, axis=-1, keepdims=True)
        p = jnp.exp(s - mx)
        denom = jnp.sum(p, axis=-1, keepdims=True)
        o = jnp.dot(p.astype(BF), v_ref[:, sl], preferred_element_type=F32)
        o_ref[:, sl] = (o / denom).astype(o_ref.dtype)


def _xattn(q, kv, ts=512):
    b, s, d = q.shape
    mem = kv.shape[1]
    return pl.pallas_call(
        functools.partial(_xattn_kernel, heads=X_HEADS, hd=d // X_HEADS),
        grid=(b, s // ts),
        in_specs=[pl.BlockSpec((None, ts, d), lambda bi, si: (bi, si, 0)),
                  pl.BlockSpec((None, mem, d), lambda bi, si: (bi, 0, 0)),
                  pl.BlockSpec((None, mem, d), lambda bi, si: (bi, 0, 1))],
        out_specs=pl.BlockSpec((None, ts, d), lambda bi, si: (bi, si, 0)),
        out_shape=jax.ShapeDtypeStruct((b, s, d), BF),
        compiler_params=_params(2),
        name="xattn",
    )(q, kv, kv)


def _causal_conv(x, tail, cw_ref, cols):
    t = x.shape[0]
    row8 = lax.broadcasted_iota(jnp.int32, (SUBLANES, x.shape[1]), 0)
    acc = x * cw_ref[CONV_W - 1:CONV_W, cols]
    for d in range(1, CONV_W):
        xr = pltpu.roll(x, d, axis=0)
        head = jnp.where(row8 < d, pltpu.roll(tail, d, axis=0), xr[0:SUBLANES])
        xs = jnp.concatenate([head, xr[SUBLANES:t]], axis=0)
        acc = acc + xs * cw_ref[CONV_W - 1 - d:CONV_W - d, cols]
    return acc


def _band_layout(c, tn, blk):
    starts, need = [], 0
    for j in range(c // tn):
        lo = (j * tn // blk) * blk
        hi = (((j + 1) * tn - 1) // blk + 1) * blk
        st = (lo // LANES) * LANES
        starts.append(st)
        need = max(need, hi - st)
    kw = -(-need // LANES) * LANES
    starts = [min(st, c - kw) for st in starts]
    return tuple(starts), kw


def _band_weights(gw, starts, kw, tn):
    dense = jax.scipy.linalg.block_diag(*[gw[i] for i in range(gw.shape[0])])
    return jnp.stack([dense[st:st + kw, j * tn:(j + 1) * tn]
                      for j, st in enumerate(starts)]).astype(BF)


def _scan_linear(a, b):
    t = a.shape[0]
    row = lax.broadcasted_iota(jnp.int32, a.shape, 0)
    d = 1
    while d < SUBLANES:
        keep = row >= d
        a_s = jnp.where(keep, pltpu.roll(a, d, axis=0), 1.0)
        b_s = jnp.where(keep, pltpu.roll(b, d, axis=0), 0.0)
        b = a * b_s + b
        a = a * a_s
        d *= 2
    while d < t:
        b = jnp.concatenate([b[:d], a[d:] * b[:t - d] + b[d:]], axis=0)
        a = jnp.concatenate([a[:d], a[d:] * a[:t - d]], axis=0)
        d *= 2
    return a, b


def _rglru_kernel(xb_ref, gb_ref, cw_ref, cb_ref, wa_ref, wx_ref, ba_ref, bx_ref, lam_ref,
                  y_ref, tail_ref, hc_ref, xc_ref, xcb_ref, *, starts, kw, tn):
    t, c = xb_ref.shape

    @pl.when(pl.program_id(1) == 0)
    def _():
        tail_ref[...] = jnp.zeros_like(tail_ref)
        hc_ref[...] = jnp.zeros_like(hc_ref)

    for j in range(c // tn):
        cols = slice(j * tn, (j + 1) * tn)
        x = xb_ref[:, cols].astype(F32)
        xc = _causal_conv(x, tail_ref[:, cols], cw_ref, cols) + cb_ref[:, cols]
        tail_ref[:, cols] = x[t - SUBLANES:t]
        xc_ref[:, cols] = xc
        xcb_ref[:, cols] = xc.astype(BF)

    for j, st in enumerate(starts):
        cols = slice(j * tn, (j + 1) * tn)
        xw = xcb_ref[:, st:st + kw]
        r = jax.nn.sigmoid(jnp.dot(xw, wa_ref[j], preferred_element_type=F32) + ba_ref[:, cols])
        i = jax.nn.sigmoid(jnp.dot(xw, wx_ref[j], preferred_element_type=F32) + bx_ref[:, cols])
        neg_lam = -lam_ref[:, cols]
        softplus = jnp.maximum(neg_lam, 0.0) + jnp.log1p(jnp.exp(-jnp.abs(neg_lam)))
        log_a = (-LRU_C) * r * softplus
        a = jnp.exp(log_a)
        th = jnp.tanh(log_a)
        b_in = jnp.sqrt(-2.0 * th / (1.0 - th)) * (i * xc_ref[:, cols])
        a_cum, h_loc = _scan_linear(a, b_in)
        h = h_loc + a_cum * hc_ref[0:1, cols]
        hc_ref[:, cols] = jnp.broadcast_to(h[t - 1:t], (SUBLANES, tn))
        y_ref[:, cols] = (h * jax.nn.gelu(gb_ref[:, cols].astype(F32))).astype(y_ref.dtype)


def _rglru(proj, conv_w, conv_b, gate_a_w, gate_a_b, gate_x_w, gate_x_b, lam, c, t=128, tn=256):
    b, s, _ = proj.shape
    starts, kw = _band_layout(c, tn, c // LRU_BLOCKS)
    wa = _band_weights(gate_a_w, starts, kw, tn)
    wx = _band_weights(gate_x_w, starts, kw, tn)
    nt = c // tn
    row = lambda v: v.reshape(1, c)
    full = lambda shape: pl.BlockSpec(shape, lambda bi, ti: (0,) * len(shape))
    return pl.pallas_call(
        functools.partial(_rglru_kernel, starts=starts, kw=kw, tn=tn),
        grid=(b, s // t),
        in_specs=[pl.BlockSpec((None, t, c), lambda bi, ti: (bi, ti, 0)),
                  pl.BlockSpec((None, t, c), lambda bi, ti: (bi, ti, 1)),
                  full((CONV_W, c)), full((1, c)),
                  full((nt, kw, tn)), full((nt, kw, tn)),
                  full((1, c)), full((1, c)), full((1, c))],
        out_specs=pl.BlockSpec((None, t, c), lambda bi, ti: (bi, ti, 0)),
        out_shape=jax.ShapeDtypeStruct((b, s, c), BF),
        scratch_shapes=[pltpu.VMEM((SUBLANES, c), F32), pltpu.VMEM((SUBLANES, c), F32),
                        pltpu.VMEM((t, c), F32), pltpu.VMEM((t, c), BF)],
        compiler_params=_params(2),
        name="rglru",
    )(proj, proj, conv_w, row(conv_b), wa, wx, row(gate_a_b), row(gate_x_b), row(lam))


def _gdn_prep_kernel(x_ref, ba_ref, cw_ref, alog_ref, dtb_ref, o_ref, bg_ref, tail_ref,
                     *, d_qk, n_heads, tc):
    t, c = x_ref.shape

    @pl.when(pl.program_id(1) == 0)
    def _():
        tail_ref[...] = jnp.zeros_like(tail_ref)

    hd = GDN_HEAD_DIM
    for j in range(c // tc):
        cols = slice(j * tc, (j + 1) * tc)
        x = x_ref[:, cols].astype(F32)
        y = _causal_conv(x, tail_ref[:, cols], cw_ref, cols)
        tail_ref[:, cols] = x[t - SUBLANES:t]
        y = y * jax.nn.sigmoid(y)
        if j * tc < 2 * d_qk:
            scale = hd ** -0.5 if j * tc < d_qk else 1.0
            for hh in range(tc // hd):
                yh = y[:, hh * hd:(hh + 1) * hd]
                ss = jnp.sum(yh * yh, axis=-1, keepdims=True)
                yn = yh * lax.rsqrt(ss + EPS)
                if scale != 1.0:
                    yn = yn * scale
                o_ref[:, j * tc + hh * hd:j * tc + (hh + 1) * hd] = yn.astype(o_ref.dtype)
        else:
            o_ref[:, cols] = y.astype(o_ref.dtype)

    ba = ba_ref[...]
    lane = lax.broadcasted_iota(jnp.int32, ba.shape, 1)
    z = ba + dtb_ref[...]
    softplus = jnp.maximum(z, 0.0) + jnp.log1p(jnp.exp(-jnp.abs(z)))
    g = -jnp.exp(alog_ref[...]) * softplus
    bg_ref[...] = jnp.where(lane < n_heads, jax.nn.sigmoid(ba), g)


def _gdn_prep(proj, ba, conv_w, a_log, dt_bias, d_qk, d_v, t=256, tc=512):
    b, s, _ = proj.shape
    c = 2 * d_qk + d_v
    nh = a_log.shape[0]
    pad = lambda v: jnp.zeros((1, LANES), F32).at[0, nh:2 * nh].set(v)
    full = lambda shape: pl.BlockSpec(shape, lambda bi, ti: (0,) * len(shape))
    return pl.pallas_call(
        functools.partial(_gdn_prep_kernel, d_qk=d_qk, n_heads=nh, tc=tc),
        grid=(b, s // t),
        in_specs=[pl.BlockSpec((None, t, c), lambda bi, ti: (bi, ti, 0)),
                  pl.BlockSpec((None, t, LANES), lambda bi, ti: (bi, ti, 0)),
                  full((CONV_W, c)), full((1, LANES)), full((1, LANES))],
        out_specs=[pl.BlockSpec((None, t, c), lambda bi, ti: (bi, ti, 0)),
                   pl.BlockSpec((None, t, LANES), lambda bi, ti: (bi, ti, 0))],
        out_shape=[jax.ShapeDtypeStruct((b, s, c), BF),
                   jax.ShapeDtypeStruct((b, s, LANES), F32)],
        scratch_shapes=[pltpu.VMEM((SUBLANES, c), F32)],
        compiler_params=_params(2),
        name="gdn_prep",
    )(proj, ba, conv_w, pad(a_log), pad(dt_bias))


def _split2(x):
    hi = x.astype(BF)
    mid = (x - hi.astype(F32)).astype(BF)
    return hi, mid


def _block_diag2(x, half):
    lane = lax.broadcasted_iota(jnp.int32, x.shape, 1)
    zero = jnp.zeros_like(x)
    return jnp.concatenate([jnp.where(lane < half, x, zero), jnp.where(lane >= half, x, zero)], axis=0)


def _alternate(first, second):
    live = [first, second]
    while live:
        for gen in list(live):
            if next(gen, "done") == "done":
                live.remove(gen)


def _gdn_delta_kernel(q_ref, k_ref, v_ref, z_ref, bg_ref, ng_ref, o_ref,
                      state_ref, w_ref, lhs_ref, attn_ref, ks_ref, dec_ref, *, n_heads, pp, cpg):
    ck = GDN_CHUNK
    hd = GDN_HEAD_DIM
    ts = q_ref.shape[0]
    n_groups = ts // (ck * cpg)
    pair0 = pl.program_id(1) * pp

    @pl.when(pl.program_id(2) == 0)
    def _():
        state_ref[...] = jnp.zeros_like(state_ref)

    lane64 = lax.broadcasted_iota(jnp.int32, (ck, 2 * ck), 1)
    row64 = lax.broadcasted_iota(jnp.int32, (ck, 2 * ck), 0)
    col_in_head = jnp.where(lane64 < ck, lane64, lane64 - ck)
    first64 = lane64 < ck
    causal = row64 >= col_in_head
    strict = row64 > col_in_head
    eye = row64 == col_in_head
    eye_f = jnp.where(eye, 1.0, 0.0)
    mask_lo = jnp.where(first64, 1.0, 0.0).astype(BF)
    mask_hi = jnp.where(first64, 0.0, 1.0).astype(BF)
    lane128 = lax.broadcasted_iota(jnp.int32, (ck, 2 * hd), 1)
    first128 = lane128 < hd
    lane_bg = lax.broadcasted_iota(jnp.int32, (ck, LANES), 1)
    dot = functools.partial(jnp.dot, preferred_element_type=F32)
    pairs = range(pp)
    insts = [(cj, pi) for cj in range(cpg) for pi in pairs]
    n_sq = 1
    while (2 << n_sq) < ck:
        n_sq += 1

    def chunk_rows(gi, cj):
        return pl.ds(pl.multiple_of((gi * cpg + cj) * ck, ck), ck)

    def bd_bf(x):
        return jnp.concatenate([x * mask_lo, x * mask_hi], axis=0)

    def precise_rhs(m_hi, m_mid):
        bd_hi = bd_bf(m_hi)
        return jnp.concatenate([bd_hi, bd_hi, bd_bf(m_mid)], axis=0)

    def precise_lhs(x_hi, x_mid):
        return jnp.concatenate([x_hi, x_mid, x_hi], axis=1)

    def prepare(gi, slot0):
        betas, gcs, qs, ks, a_mats = {}, {}, {}, {}, {}
        for cj, pi in insts:
            rows = chunk_rows(gi, cj)
            bg = bg_ref[rows, :]

            def column(l, bg=bg):
                return jnp.sum(jnp.where(lane_bg == l, bg, 0.0), axis=-1, keepdims=True)

            h0 = 2 * (pair0 + pi)
            beta0, beta1 = column(h0), column(h0 + 1)
            gc = jnp.where(first64, column(n_heads + h0), column(n_heads + h0 + 1))
            d = 1
            while d < ck:
                if d < SUBLANES:
                    gc = gc + jnp.where(row64 >= d, pltpu.roll(gc, d, axis=0), 0.0)
                else:
                    gc = jnp.concatenate([gc[:d], gc[d:] + gc[:ck - d]], axis=0)
                d *= 2
            g_row = jnp.sum(jnp.where(eye, gc, 0.0), axis=0, keepdims=True)
            decay = jnp.where(causal, jnp.exp(jnp.where(causal, gc - g_row, 0.0)), 0.0)

            q = q_ref[rows, pi * hd:(pi + 1) * hd]
            k = k_ref[rows, pi * hd:(pi + 1) * hd]
            qk = jnp.concatenate([q, k], axis=0)
            kk = jnp.concatenate([k, k], axis=0)
            prod = lax.dot_general(qk, kk, (((1,), (1,)), ((), ())), preferred_element_type=F32)
            beta64 = jnp.where(first64, beta0, beta1)
            a_mats[cj, pi] = jnp.where(strict, prod[ck:] * beta64 * decay, 0.0)
            attn_ref[slot0 + cj, pi] = jnp.where(causal, prod[:ck] * decay, 0.0).astype(BF)
            betas[cj, pi] = (beta0, beta1)
            gcs[cj, pi] = gc
            qs[cj, pi] = q
            ks[cj, pi] = k
        yield

        p_mats, m_mats = {}, {}
        for key in insts:
            a_hi, a_mid = _split2(a_mats[key])
            m_mats[key] = dot(precise_lhs(a_hi, a_mid), precise_rhs(a_hi, a_mid))
            p_mats[key] = eye_f - a_mats[key]
        yield
        for step in range(n_sq):
            for key in insts:
                m_hi, m_mid = _split2(m_mats[key])
                p_hi, p_mid = _split2(p_mats[key])
                rhs = precise_rhs(m_hi, m_mid)
                if step + 1 < n_sq:
                    lhs = jnp.concatenate([precise_lhs(p_hi, p_mid), precise_lhs(m_hi, m_mid)], axis=0)
                    both = dot(lhs, rhs)
                    p_mats[key] = p_mats[key] + both[:ck]
                    m_mats[key] = both[ck:]
                else:
                    p_mats[key] = p_mats[key] + dot(precise_lhs(p_hi, p_mid), rhs)
            yield

        for cj, pi in insts:
            rows = chunk_rows(gi, cj)
            t_inv = p_mats[cj, pi]
            gc = gcs[cj, pi]
            gc128 = jnp.where(first128, gc[:, 0:1], gc[:, ck:ck + 1])
            gl128 = gc128[ck - 1:ck, :]
            beta128 = jnp.where(first128, betas[cj, pi][0], betas[cj, pi][1])
            e_g = jnp.exp(gc128)
            kf = ks[cj, pi].astype(F32)
            k2 = jnp.concatenate([kf, kf], axis=1)
            qf = qs[cj, pi].astype(F32)
            q2 = jnp.concatenate([qf, qf], axis=1)
            vb = v_ref[rows, pi * 2 * hd:(pi + 1) * 2 * hd].astype(F32) * beta128
            kbg = k2 * beta128 * e_g
            rhs = jnp.concatenate([_block_diag2(vb, hd), _block_diag2(kbg, hd)], axis=1).astype(BF)
            wk = dot(t_inv.astype(BF), rhs)
            slot = slot0 + cj
            w_ref[slot, pi] = wk[:, :2 * hd]
            lhs_ref[slot, pi] = jnp.concatenate([wk[:, 2 * hd:], q2 * e_g], axis=0).astype(BF)
            ks_ref[slot, pi] = _block_diag2(k2 * jnp.exp(gl128 - gc128), hd).astype(BF)
            dec_ref[slot, pi] = jnp.broadcast_to(jnp.exp(gl128), (SUBLANES, 2 * hd))
        yield

    def recur(gi, slot0):
        ng = ng_ref[...]
        for cj in range(cpg):
            slot = slot0 + cj
            rows = chunk_rows(gi, cj)
            states = [state_ref[pi] for pi in pairs]
            ks_qs = [dot(lhs_ref[slot, pi], states[pi].astype(BF)) for pi in pairs]
            yield
            v_bds = [_block_diag2(w_ref[slot, pi] - ks_qs[pi][:ck], hd).astype(BF) for pi in pairs]
            upds = [lax.dot_general(ks_ref[slot, pi], v_bds[pi], (((0,), (0,)), ((), ())),
                                    preferred_element_type=F32) for pi in pairs]
            for pi in pairs:
                state_ref[pi] = states[pi] * dec_ref[slot, pi, 0:1, :] + upds[pi]
            yield
            outs = [ks_qs[pi][ck:] + dot(attn_ref[slot, pi], v_bds[pi]) for pi in pairs]
            for pi in pairs:
                for hh in range(2):
                    sl = slice((2 * pi + hh) * hd, (2 * pi + hh + 1) * hd)
                    oh = outs[pi][:, hh * hd:(hh + 1) * hd]
                    ms = jnp.mean(oh * oh, axis=-1, keepdims=True)
                    zh = z_ref[rows, sl].astype(F32)
                    gate = zh * jax.nn.sigmoid(zh)
                    o_ref[rows, sl] = (oh * lax.rsqrt(ms + EPS) * ng * gate).astype(o_ref.dtype)
            yield

    for _ in prepare(0, 0):
        pass

    def body(g2, carry):
        g_even = 2 * g2
        _alternate(recur(g_even, 0), prepare(g_even + 1, cpg))
        _alternate(recur(g_even + 1, cpg), prepare(jnp.minimum(g_even + 2, n_groups - 1), 0))
        return carry

    lax.fori_loop(0, n_groups // 2, body, 0)


def _gdn_delta(qkv, zsrc, z_col0, bg, norm_g, d_qk, d_v, ts=2048, pp=4, cpg=2):
    b, s, _ = qkv.shape
    hd = GDN_HEAD_DIM
    ck = GDN_CHUNK
    n_heads = d_v // hd
    n_pairs = n_heads // 2
    ts = min(ts, s)
    assert d_qk // hd == n_pairs and n_pairs % pp == 0 and s % ts == 0 and ts % (2 * cpg * ck) == 0
    qw = pp * hd
    vw = 2 * pp * hd
    assert d_qk % qw == 0 and (2 * d_qk) % vw == 0 and z_col0 % vw == 0
    k_blk0 = d_qk // qw
    v_blk0 = 2 * d_qk // vw
    z_blk0 = z_col0 // vw
    n_slots = 2 * cpg
    return pl.pallas_call(
        functools.partial(_gdn_delta_kernel, n_heads=n_heads, pp=pp, cpg=cpg),
        grid=(b, n_pairs // pp, s // ts),
        in_specs=[pl.BlockSpec((None, ts, qw), lambda bi, pi, si: (bi, si, pi)),
                  pl.BlockSpec((None, ts, qw), lambda bi, pi, si: (bi, si, k_blk0 + pi)),
                  pl.BlockSpec((None, ts, vw), lambda bi, pi, si: (bi, si, v_blk0 + pi)),
                  pl.BlockSpec((None, ts, vw), lambda bi, pi, si: (bi, si, z_blk0 + pi)),
                  pl.BlockSpec((None, ts, LANES), lambda bi, pi, si: (bi, si, 0)),
                  pl.BlockSpec((1, hd), lambda bi, pi, si: (0, 0))],
        out_specs=pl.BlockSpec((None, ts, vw), lambda bi, pi, si: (bi, si, pi)),
        out_shape=jax.ShapeDtypeStruct((b, s, d_v), BF),
        scratch_shapes=[pltpu.VMEM((pp, 2 * hd, 2 * hd), F32),
                        pltpu.VMEM((n_slots, pp, ck, 2 * hd), F32),
                        pltpu.VMEM((n_slots, pp, 2 * ck, 2 * hd), BF),
                        pltpu.VMEM((n_slots, pp, ck, 2 * ck), BF),
                        pltpu.VMEM((n_slots, pp, 2 * ck, 2 * hd), BF),
                        pltpu.VMEM((n_slots, pp, SUBLANES, 2 * hd), F32)],
        compiler_params=_params(3),
        name="gdn_delta",
    )(qkv, qkv, qkv, zsrc, bg, norm_g.reshape(1, hd))


def kernel(x, mem, norm_mix_g, norm_mem_g, mem_kv_w, norm_ffn_g, ffn_w_in, ffn_w_out, lru_w_in, lru_w_out, lru_conv_w, lru_conv_b, lru_gate_a_w, lru_gate_a_b, lru_gate_x_w, lru_gate_x_b, lru_lambda, gdn_w_in, gdn_w_out, gdn_conv_w, gdn_a_log, gdn_dt_bias, gdn_norm_g, final_norm_g):
    b, s, d = x.shape
    m = b * s
    mem_len = mem.shape[1]
    depth = norm_mix_g.shape[0]
    d_rnn = lru_lambda.shape[1]
    n_vh = gdn_a_log.shape[1]
    d_v = n_vh * GDN_HEAD_DIM
    d_qk = (gdn_conv_w.shape[2] - d_v) // 2
    gdn_main = 2 * d_qk + 2 * d_v
    tail_blocks = -(-(gdn_w_in.shape[2] - gdn_main) // LANES)

    h = x.reshape(m, d)
    mem2 = mem.reshape(b * mem_len, d)
    for i in range(depth):
        j = i // 2
        hn = _rmsnorm(h, norm_mix_g[i], BF)
        mem_n = _rmsnorm(mem2, norm_mem_g[i], BF)
        kv = _matmul([mem_n], mem_kv_w, i, 0, 2 * d, BF).reshape(b, mem_len, 2 * d)
        if i % 2 == 0:
            proj = _matmul([hn], lru_w_in, j, 0, 2 * d_rnn, BF, tm=1024).reshape(b, s, 2 * d_rnn)
            xq = _matmul([hn], lru_w_in, j, 2 * d_rnn, d, BF, tm=1024)
            y = _rglru(proj, lru_conv_w[j], lru_conv_b[j], lru_gate_a_w[j], lru_gate_a_b[j],
                       lru_gate_x_w[j], lru_gate_x_b[j], lru_lambda[j], d_rnn)
            w_out = lru_w_out
        else:
            proj = _matmul([hn], gdn_w_in, j, 0, gdn_main, BF, tm=1024).reshape(b, s, gdn_main)
            w_tail = _column_window(gdn_w_in, j, gdn_main, tail_blocks)
            w_ba = jnp.pad(w_tail[:, :2 * n_vh], ((0, 0), (0, LANES - 2 * n_vh)))[None]
            ba = _matmul([hn], w_ba, 0, 0, LANES, F32, tm=1024).reshape(b, s, LANES)
            xq = _matmul([hn], w_tail[None, :, 2 * n_vh:2 * n_vh + d], 0, 0, d, BF, tm=1024)
            qkv, bg = _gdn_prep(proj, ba, gdn_conv_w[j], gdn_a_log[j], gdn_dt_bias[j], d_qk, d_v)
            y = _gdn_delta(qkv, proj, 2 * d_qk + d_v, bg, gdn_norm_g[j], d_qk, d_v)
            w_out = gdn_w_out
        xo = _xattn(xq.reshape(b, s, d), kv)
        y2 = y.reshape(m, y.shape[-1])
        h = _matmul([y2, xo.reshape(m, d)], w_out, j, 0, d, F32, residual=h)
        hn = _rmsnorm(h, norm_ffn_g[i], BF)
        act = _ffn_in(hn, ffn_w_in, i)
        h = _matmul([act], ffn_w_out, i, 0, d, F32, residual=h)
    return _rmsnorm(h, final_norm_g, F32).reshape(b, s, d)
```

```python
import functools

import jax
import jax.numpy as jnp
from jax import lax
from jax.experimental import pallas as pl
from jax.experimental.pallas import tpu as pltpu

F32 = jnp.float32
BF = jnp.bfloat16

EPS = 1e-6
CONV_W = 4
LRU_BLOCKS = 16
LRU_C = 8.0
GDN_HEAD_DIM = 128
GDN_CHUNK = 64
X_HEADS = 4

LANES = 128
SUBLANES = 8
VMEM_LIMIT = 56 * 1024 * 1024


def _params(n_axes):
    return pltpu.CompilerParams(
        dimension_semantics=("arbitrary",) * n_axes, vmem_limit_bytes=VMEM_LIMIT)


def _rmsnorm_kernel(x_ref, g_ref, o_ref):
    x = x_ref[...]
    ms = jnp.mean(x * x, axis=-1, keepdims=True)
    o_ref[...] = (x * lax.rsqrt(ms + EPS) * g_ref[...]).astype(o_ref.dtype)


def _rmsnorm(x, g, out_dtype, tm=512):
    m, d = x.shape
    tm = min(tm, m)
    return pl.pallas_call(
        _rmsnorm_kernel,
        grid=(m // tm,),
        in_specs=[pl.BlockSpec((tm, d), lambda i: (i, 0)),
                  pl.BlockSpec((1, d), lambda i: (0, 0))],
        out_specs=pl.BlockSpec((tm, d), lambda i: (i, 0)),
        out_shape=jax.ShapeDtypeStruct((m, d), out_dtype),
        compiler_params=_params(1),
        name="rmsnorm",
    )(x, g.reshape(1, d))


_CAST_ROWS = 512


def _cast_weight(w_ref, wb_ref, g_ref=None):
    k = w_ref.shape[0]
    rows = _CAST_ROWS if k % _CAST_ROWS == 0 else k

    def body(r, carry):
        sl = pl.ds(pl.multiple_of(r * rows, rows), rows)
        blk = w_ref[sl, :]
        if g_ref is not None:
            blk = blk * g_ref[sl, :]
        wb_ref[sl, :] = blk.astype(BF)
        return carry

    lax.fori_loop(0, k // rows, body, 0)


def _cast_weight_t(wt_ref, wb_ref, g_ref=None):
    k = wt_ref.shape[1]
    step = _CAST_ROWS if k % _CAST_ROWS == 0 else k
    for r in range(k // step):
        sl = slice(r * step, (r + 1) * step)
        blk = wt_ref[:, sl].T
        if g_ref is not None:
            blk = blk * g_ref[sl, :]
        wb_ref[sl, :] = blk.astype(BF)


def _row_rms_scale(x_ref, r_ref, rows):
    @pl.when(pl.program_id(0) == 0)
    def _():
        xf = x_ref[...].astype(F32)
        r_ref[rows, :] = lax.rsqrt(jnp.mean(xf * xf, axis=-1, keepdims=True) + EPS)


def _mm_kernel(*refs, k_sizes, has_res, w_t, norm, emit_bf16):
    n_x = len(k_sizes)
    x_refs = refs[:n_x]
    pos = n_x
    w_ref = refs[pos]
    pos += 1
    res_ref = refs[pos] if has_res else None
    pos += has_res
    g_ref = refs[pos] if norm else None
    pos += norm
    o_ref = refs[pos]
    pos += 1
    ob_ref = refs[pos] if emit_bf16 else None
    pos += emit_bf16
    wb_ref = refs[pos]
    r_ref = refs[pos + 1] if norm else None
    tm = o_ref.shape[0]
    rows = pl.ds(pl.multiple_of(pl.program_id(1) * tm, tm), tm)

    @pl.when(pl.program_id(1) == 0)
    def _():
        if w_t:
            _cast_weight_t(w_ref, wb_ref, g_ref)
        else:
            _cast_weight(w_ref, wb_ref, g_ref)

    if norm:
        _row_rms_scale(x_refs[0], r_ref, rows)

    acc = None
    off = 0
    for x_ref, k in zip(x_refs, k_sizes):
        part = jnp.dot(x_ref[...], wb_ref[off:off + k, :], preferred_element_type=F32)
        acc = part if acc is None else acc + part
        off += k
    if norm:
        acc = acc * r_ref[rows, :]
    if has_res:
        acc = acc + res_ref[...]
    o_ref[...] = acc.astype(o_ref.dtype)
    if emit_bf16:
        ob_ref[...] = acc.astype(BF)


def _matmul(xs, w, layer, col0, n_cols, out_dtype, residual=None, tm=512, tn=512, w_t=False,
            gain=None, emit_bf16=False):
    m = xs[0].shape[0]
    k_sizes = tuple(x.shape[1] for x in xs)
    k_tot = sum(k_sizes)
    norm = gain is not None
    assert w.shape[2 if w_t else 1] == k_tot and not (norm and (residual is not None or len(xs) > 1))
    tm = min(tm, m)
    tn = min(tn, n_cols)
    assert m % tm == 0 and n_cols % tn == 0 and col0 % tn == 0
    cb0 = col0 // tn
    in_specs = [pl.BlockSpec((tm, k), lambda j, i: (i, 0)) for k in k_sizes]
    if w_t:
        in_specs.append(pl.BlockSpec((None, tn, k_tot), lambda j, i: (layer, j + cb0, 0)))
    else:
        in_specs.append(pl.BlockSpec((None, k_tot, tn), lambda j, i: (layer, 0, j + cb0)))
    args = list(xs) + [w]
    if residual is not None:
        in_specs.append(pl.BlockSpec((tm, tn), lambda j, i: (i, j)))
        args.append(residual)
    scratch = [pltpu.VMEM((k_tot, tn), BF)]
    if norm:
        in_specs.append(pl.BlockSpec((k_tot, 1), lambda j, i: (0, 0)))
        args.append(gain.reshape(k_tot, 1))
        scratch.append(pltpu.VMEM((m, 1), F32))
    out_spec = pl.BlockSpec((tm, tn), lambda j, i: (i, j))
    out_shape = jax.ShapeDtypeStruct((m, n_cols), out_dtype)
    return pl.pallas_call(
        functools.partial(_mm_kernel, k_sizes=k_sizes, has_res=residual is not None, w_t=w_t,
                          norm=norm, emit_bf16=emit_bf16),
        grid=(n_cols // tn, m // tm),
        in_specs=in_specs,
        out_specs=[out_spec, out_spec] if emit_bf16 else out_spec,
        out_shape=[out_shape, jax.ShapeDtypeStruct((m, n_cols), BF)] if emit_bf16 else out_shape,
        scratch_shapes=scratch,
        compiler_params=_params(2),
        name="matmul",
    )(*args)


def _copy_kernel(w_ref, o_ref):
    o_ref[...] = w_ref[...]


def _row_window(w, layer, row0, n_blocks):
    _, _, k = w.shape
    assert row0 % LANES == 0
    rb0 = row0 // LANES
    return pl.pallas_call(
        _copy_kernel,
        grid=(n_blocks,),
        in_specs=[pl.BlockSpec((None, LANES, k), lambda j: (layer, j + rb0, 0))],
        out_specs=pl.BlockSpec((LANES, k), lambda j: (j, 0)),
        out_shape=jax.ShapeDtypeStruct((n_blocks * LANES, k), w.dtype),
        compiler_params=_params(1),
        name="row_window",
    )(w)


def _ffn_in_kernel(x_ref, wg_ref, wu_ref, g_ref, o_ref, wgb_ref, wub_ref, r_ref):
    tm = o_ref.shape[0]
    rows = pl.ds(pl.multiple_of(pl.program_id(1) * tm, tm), tm)

    @pl.when(pl.program_id(1) == 0)
    def _():
        _cast_weight(wg_ref, wgb_ref, g_ref)
        _cast_weight(wu_ref, wub_ref, g_ref)

    _row_rms_scale(x_ref, r_ref, rows)
    x = x_ref[...]
    r = r_ref[rows, :]
    gate = jnp.dot(x, wgb_ref[...], preferred_element_type=F32) * r
    up = jnp.dot(x, wub_ref[...], preferred_element_type=F32) * r
    o_ref[...] = (gate * jax.nn.sigmoid(gate) * up).astype(o_ref.dtype)


def _ffn_in(x, w, layer, gain, tm=1024, tn=512):
    m, k = x.shape
    f = w.shape[2] // 2
    assert m % tm == 0 and f % tn == 0
    nb = f // tn
    return pl.pallas_call(
        _ffn_in_kernel,
        grid=(nb, m // tm),
        in_specs=[pl.BlockSpec((tm, k), lambda j, i: (i, 0)),
                  pl.BlockSpec((None, k, tn), lambda j, i: (layer, 0, j)),
                  pl.BlockSpec((None, k, tn), lambda j, i: (layer, 0, j + nb)),
                  pl.BlockSpec((k, 1), lambda j, i: (0, 0))],
        out_specs=pl.BlockSpec((tm, tn), lambda j, i: (i, j)),
        out_shape=jax.ShapeDtypeStruct((m, f), BF),
        scratch_shapes=[pltpu.VMEM((k, tn), BF), pltpu.VMEM((k, tn), BF), pltpu.VMEM((m, 1), F32)],
        compiler_params=_params(2),
        name="ffn_in",
    )(x, w, w, gain.reshape(k, 1))


def _xattn_kernel(q_ref, k_ref, v_ref, o_ref, *, heads, hd):
    scale = hd ** -0.5
    for h in range(heads):
        sl = slice(h * hd, (h + 1) * hd)
        s = lax.dot_general(q_ref[:, sl], k_ref[:, sl], (((1,), (1,)), ((), ())),
                            preferred_element_type=F32) * scale
        mx = jnp.max(s, axis=-1, keepdims=True)
        p = jnp.exp(s - mx)
        denom = jnp.sum(p, axis=-1, keepdims=True)
        o = jnp.dot(p.astype(BF), v_ref[:, sl], preferred_element_type=F32)
        o_ref[:, sl] = (o / denom).astype(o_ref.dtype)


def _xattn(q, kv, ts=512):
    b, s, d = q.shape
    mem = kv.shape[1]
    return pl.pallas_call(
        functools.partial(_xattn_kernel, heads=X_HEADS, hd=d // X_HEADS),
        grid=(b, s // ts),
        in_specs=[pl.BlockSpec((None, ts, d), lambda bi, si: (bi, si, 0)),
                  pl.BlockSpec((None, mem, d), lambda bi, si: (bi, 0, 0)),
                  pl.BlockSpec((None, mem, d), lambda bi, si: (bi, 0, 1))],
        out_specs=pl.BlockSpec((None, ts, d), lambda bi, si: (bi, si, 0)),
        out_shape=jax.ShapeDtypeStruct((b, s, d), BF),
        compiler_params=_params(2),
        name="xattn",
    )(q, kv, kv)


def _causal_conv(x, tail, cw_ref, cols):
    t = x.shape[0]
    row8 = lax.broadcasted_iota(jnp.int32, (SUBLANES, x.shape[1]), 0)
    acc = x * cw_ref[CONV_W - 1:CONV_W, cols]
    for d in range(1, CONV_W):
        xr = pltpu.roll(x, d, axis=0)
        head = jnp.where(row8 < d, pltpu.roll(tail, d, axis=0), xr[0:SUBLANES])
        xs = jnp.concatenate([head, xr[SUBLANES:t]], axis=0)
        acc = acc + xs * cw_ref[CONV_W - 1 - d:CONV_W - d, cols]
    return acc


def _band_layout(c, tn, blk):
    starts, need = [], 0
    for j in range(c // tn):
        lo = (j * tn // blk) * blk
        hi = (((j + 1) * tn - 1) // blk + 1) * blk
        st = (lo // LANES) * LANES
        starts.append(st)
        need = max(need, hi - st)
    kw = -(-need // LANES) * LANES
    starts = [min(st, c - kw) for st in starts]
    return tuple(starts), kw


def _band_weights(gw, starts, kw, tn):
    dense = jax.scipy.linalg.block_diag(*[gw[i] for i in range(gw.shape[0])])
    return jnp.stack([dense[st:st + kw, j * tn:(j + 1) * tn]
                      for j, st in enumerate(starts)]).astype(BF)


def _scan_linear(a, b):
    t = a.shape[0]
    row = lax.broadcasted_iota(jnp.int32, a.shape, 0)
    d = 1
    while d < SUBLANES:
        keep = row >= d
        a_s = jnp.where(keep, pltpu.roll(a, d, axis=0), 1.0)
        b_s = jnp.where(keep, pltpu.roll(b, d, axis=0), 0.0)
        b = a * b_s + b
        a = a * a_s
        d *= 2
    while d < t:
        b = jnp.concatenate([b[:d], a[d:] * b[:t - d] + b[d:]], axis=0)
        a = jnp.concatenate([a[:d], a[d:] * a[:t - d]], axis=0)
        d *= 2
    return a, b


def _rglru_kernel(xb_ref, gb_ref, cw_ref, cb_ref, wa_ref, wx_ref, ba_ref, bx_ref, lam_ref,
                  y_ref, tail_ref, hc_ref, xc_ref, xcb_ref, *, starts, kw, tn):
    t, c = xb_ref.shape

    @pl.when(pl.program_id(1) == 0)
    def _():
        tail_ref[...] = jnp.zeros_like(tail_ref)
        hc_ref[...] = jnp.zeros_like(hc_ref)

    for j in range(c // tn):
        cols = slice(j * tn, (j + 1) * tn)
        x = xb_ref[:, cols].astype(F32)
        xc = _causal_conv(x, tail_ref[:, cols], cw_ref, cols) + cb_ref[:, cols]
        tail_ref[:, cols] = x[t - SUBLANES:t]
        xc_ref[:, cols] = xc
        xcb_ref[:, cols] = xc.astype(BF)

    for j, st in enumerate(starts):
        cols = slice(j * tn, (j + 1) * tn)
        xw = xcb_ref[:, st:st + kw]
        r = jax.nn.sigmoid(jnp.dot(xw, wa_ref[j], preferred_element_type=F32) + ba_ref[:, cols])
        i = jax.nn.sigmoid(jnp.dot(xw, wx_ref[j], preferred_element_type=F32) + bx_ref[:, cols])
        neg_lam = -lam_ref[:, cols]
        softplus = jnp.maximum(neg_lam, 0.0) + jnp.log1p(jnp.exp(-jnp.abs(neg_lam)))
        log_a = (-LRU_C) * r * softplus
        a = jnp.exp(log_a)
        th = jnp.tanh(log_a)
        b_in = jnp.sqrt(-2.0 * th / (1.0 - th)) * (i * xc_ref[:, cols])
        a_cum, h_loc = _scan_linear(a, b_in)
        h = h_loc + a_cum * hc_ref[0:1, cols]
        hc_ref[:, cols] = jnp.broadcast_to(h[t - 1:t], (SUBLANES, tn))
        y_ref[:, cols] = (h * jax.nn.gelu(gb_ref[:, cols].astype(F32))).astype(y_ref.dtype)


def _rglru(proj, conv_w, conv_b, gate_a_w, gate_a_b, gate_x_w, gate_x_b, lam, c, t=128, tn=256):
    b, s, _ = proj.shape
    starts, kw = _band_layout(c, tn, c // LRU_BLOCKS)
    wa = _band_weights(gate_a_w, starts, kw, tn)
    wx = _band_weights(gate_x_w, starts, kw, tn)
    nt = c // tn
    row = lambda v: v.reshape(1, c)
    full = lambda shape: pl.BlockSpec(shape, lambda bi, ti: (0,) * len(shape))
    return pl.pallas_call(
        functools.partial(_rglru_kernel, starts=starts, kw=kw, tn=tn),
        grid=(b, s // t),
        in_specs=[pl.BlockSpec((None, t, c), lambda bi, ti: (bi, ti, 0)),
                  pl.BlockSpec((None, t, c), lambda bi, ti: (bi, ti, 1)),
                  full((CONV_W, c)), full((1, c)),
                  full((nt, kw, tn)), full((nt, kw, tn)),
                  full((1, c)), full((1, c)), full((1, c))],
        out_specs=pl.BlockSpec((None, t, c), lambda bi, ti: (bi, ti, 0)),
        out_shape=jax.ShapeDtypeStruct((b, s, c), BF),
        scratch_shapes=[pltpu.VMEM((SUBLANES, c), F32), pltpu.VMEM((SUBLANES, c), F32),
                        pltpu.VMEM((t, c), F32), pltpu.VMEM((t, c), BF)],
        compiler_params=_params(2),
        name="rglru",
    )(proj, proj, conv_w, row(conv_b), wa, wx, row(gate_a_b), row(gate_x_b), row(lam))


def _gdn_prep_kernel(x_ref, ba_ref, cw_ref, alog_ref, dtb_ref, o_ref, bg_ref, tail_ref,
                     *, d_qk, n_heads, tc):
    t, c = x_ref.shape

    @pl.when(pl.program_id(1) == 0)
    def _():
        tail_ref[...] = jnp.zeros_like(tail_ref)

    hd = GDN_HEAD_DIM
    for j in range(c // tc):
        cols = slice(j * tc, (j + 1) * tc)
        x = x_ref[:, cols].astype(F32)
        y = _causal_conv(x, tail_ref[:, cols], cw_ref, cols)
        tail_ref[:, cols] = x[t - SUBLANES:t]
        y = y * jax.nn.sigmoid(y)
        if j * tc < 2 * d_qk:
            scale = hd ** -0.5 if j * tc < d_qk else 1.0
            for hh in range(tc // hd):
                yh = y[:, hh * hd:(hh + 1) * hd]
                ss = jnp.sum(yh * yh, axis=-1, keepdims=True)
                yn = yh * lax.rsqrt(ss + EPS)
                if scale != 1.0:
                    yn = yn * scale
                o_ref[:, j * tc + hh * hd:j * tc + (hh + 1) * hd] = yn.astype(o_ref.dtype)
        else:
            o_ref[:, cols] = y.astype(o_ref.dtype)

    ba = ba_ref[...]
    lane = lax.broadcasted_iota(jnp.int32, ba.shape, 1)
    z = ba + dtb_ref[...]
    softplus = jnp.maximum(z, 0.0) + jnp.log1p(jnp.exp(-jnp.abs(z)))
    g = -jnp.exp(alog_ref[...]) * softplus
    bg_ref[...] = jnp.where(lane < n_heads, jax.nn.sigmoid(ba), g)


def _gdn_prep(proj, ba, conv_w, a_log, dt_bias, d_qk, d_v, t=256, tc=512):
    b, s, _ = proj.shape
    c = 2 * d_qk + d_v
    nh = a_log.shape[0]
    pad = lambda v: jnp.zeros((1, LANES), F32).at[0, nh:2 * nh].set(v)
    full = lambda shape: pl.BlockSpec(shape, lambda bi, ti: (0,) * len(shape))
    return pl.pallas_call(
        functools.partial(_gdn_prep_kernel, d_qk=d_qk, n_heads=nh, tc=tc),
        grid=(b, s // t),
        in_specs=[pl.BlockSpec((None, t, c), lambda bi, ti: (bi, ti, 0)),
                  pl.BlockSpec((None, t, LANES), lambda bi, ti: (bi, ti, 0)),
                  full((CONV_W, c)), full((1, LANES)), full((1, LANES))],
        out_specs=[pl.BlockSpec((None, t, c), lambda bi, ti: (bi, ti, 0)),
                   pl.BlockSpec((None, t, LANES), lambda bi, ti: (bi, ti, 0))],
        out_shape=[jax.ShapeDtypeStruct((b, s, c), BF),
                   jax.ShapeDtypeStruct((b, s, LANES), F32)],
        scratch_shapes=[pltpu.VMEM((SUBLANES, c), F32)],
        compiler_params=_params(2),
        name="gdn_prep",
    )(proj, ba, conv_w, pad(a_log), pad(dt_bias))


def _split2(x):
    hi = x.astype(BF)
    mid = (x - hi.astype(F32)).astype(BF)
    return hi, mid


def _block_diag2(x, half):
    lane = lax.broadcasted_iota(jnp.int32, x.shape, 1)
    zero = jnp.zeros_like(x)
    return jnp.concatenate([jnp.where(lane < half, x, zero), jnp.where(lane >= half, x, zero)], axis=0)


def _alternate(first, second):
    live = [first, second]
    while live:
        for gen in list(live):
            if next(gen, "done") == "done":
                live.remove(gen)


def _gdn_delta_kernel(q_ref, k_ref, v_ref, z_ref, bg_ref, ng_ref, o_ref,
                      state_ref, w_ref, lhs_ref, attn_ref, ks_ref, dec_ref, *, n_heads, pp, cpg):
    ck = GDN_CHUNK
    hd = GDN_HEAD_DIM
    ts = q_ref.shape[0]
    n_groups = ts // (ck * cpg)
    pair0 = pl.program_id(1) * pp

    @pl.when(pl.program_id(2) == 0)
    def _():
        state_ref[...] = jnp.zeros_like(state_ref)

    lane64 = lax.broadcasted_iota(jnp.int32, (ck, 2 * ck), 1)
    row64 = lax.broadcasted_iota(jnp.int32, (ck, 2 * ck), 0)
    col_in_head = jnp.where(lane64 < ck, lane64, lane64 - ck)
    first64 = lane64 < ck
    causal = row64 >= col_in_head
    strict = row64 > col_in_head
    eye = row64 == col_in_head
    eye_f = jnp.where(eye, 1.0, 0.0)
    mask_lo = jnp.where(first64, 1.0, 0.0).astype(BF)
    mask_hi = jnp.where(first64, 0.0, 1.0).astype(BF)
    lane128 = lax.broadcasted_iota(jnp.int32, (ck, 2 * hd), 1)
    first128 = lane128 < hd
    lane_bg = lax.broadcasted_iota(jnp.int32, (ck, LANES), 1)
    dot = functools.partial(jnp.dot, preferred_element_type=F32)
    pairs = range(pp)
    insts = [(cj, pi) for cj in range(cpg) for pi in pairs]
    n_sq = 1
    while (2 << n_sq) < ck:
        n_sq += 1

    def chunk_rows(gi, cj):
        return pl.ds(pl.multiple_of((gi * cpg + cj) * ck, ck), ck)

    def bd_bf(x):
        return jnp.concatenate([x * mask_lo, x * mask_hi], axis=0)

    def precise_rhs(m_hi, m_mid):
        bd_hi = bd_bf(m_hi)
        return jnp.concatenate([bd_hi, bd_hi, bd_bf(m_mid)], axis=0)

    def precise_lhs(x_hi, x_mid):
        return jnp.concatenate([x_hi, x_mid, x_hi], axis=1)

    def prepare(gi, slot0):
        betas, gcs, qs, ks, a_mats = {}, {}, {}, {}, {}
        for cj, pi in insts:
            rows = chunk_rows(gi, cj)
            bg = bg_ref[rows, :]

            def column(l, bg=bg):
                return jnp.sum(jnp.where(lane_bg == l, bg, 0.0), axis=-1, keepdims=True)

            h0 = 2 * (pair0 + pi)
            beta0, beta1 = column(h0), column(h0 + 1)
            gc = jnp.where(first64, column(n_heads + h0), column(n_heads + h0 + 1))
            d = 1
            while d < ck:
                if d < SUBLANES:
                    gc = gc + jnp.where(row64 >= d, pltpu.roll(gc, d, axis=0), 0.0)
                else:
                    gc = jnp.concatenate([gc[:d], gc[d:] + gc[:ck - d]], axis=0)
                d *= 2
            g_row = jnp.sum(jnp.where(eye, gc, 0.0), axis=0, keepdims=True)
            decay = jnp.where(causal, jnp.exp(jnp.where(causal, gc - g_row, 0.0)), 0.0)

            q = q_ref[rows, pi * hd:(pi + 1) * hd]
            k = k_ref[rows, pi * hd:(pi + 1) * hd]
            qk = jnp.concatenate([q, k], axis=0)
            kk = jnp.concatenate([k, k], axis=0)
            prod = lax.dot_general(qk, kk, (((1,), (1,)), ((), ())), preferred_element_type=F32)
            beta64 = jnp.where(first64, beta0, beta1)
            a_mats[cj, pi] = jnp.where(strict, prod[ck:] * beta64 * decay, 0.0)
            attn_ref[slot0 + cj, pi] = jnp.where(causal, prod[:ck] * decay, 0.0).astype(BF)
            betas[cj, pi] = (beta0, beta1)
            gcs[cj, pi] = gc
            qs[cj, pi] = q
            ks[cj, pi] = k
        yield

        p_mats, m_mats, a_split = {}, {}, {}
        for key in insts:
            a_split[key] = _split2(a_mats[key])
            a_hi = a_split[key][0]
            m_mats[key] = dot(a_hi, bd_bf(a_hi))
            p_mats[key] = eye_f - a_mats[key]
        yield
        for step in range(n_sq):
            for key in insts:
                m_bf = m_mats[key].astype(BF)
                p_bf = p_mats[key].astype(BF)
                if step + 1 < n_sq:
                    both = dot(jnp.concatenate([p_bf, m_bf], axis=0), bd_bf(m_bf))
                    p_mats[key] = p_mats[key] + both[:ck]
                    m_mats[key] = both[ck:]
                else:
                    p_mats[key] = p_mats[key] + dot(p_bf, bd_bf(m_bf))
            yield
        t_his, resid = {}, {}
        for key in insts:
            t_hi, t_mid = _split2(p_mats[key])
            a_hi, a_mid = a_split[key]
            a_t = dot(precise_lhs(a_hi, a_mid), precise_rhs(t_hi, t_mid))
            resid[key] = (eye_f - p_mats[key]) - a_t
            t_his[key] = t_hi
        yield
        for key in insts:
            p_mats[key] = p_mats[key] + dot(t_his[key], bd_bf(resid[key].astype(BF)))
        yield

        for cj, pi in insts:
            rows = chunk_rows(gi, cj)
            t_inv = p_mats[cj, pi]
            gc = gcs[cj, pi]
            gc128 = jnp.where(first128, gc[:, 0:1], gc[:, ck:ck + 1])
            gl128 = gc128[ck - 1:ck, :]
            beta128 = jnp.where(first128, betas[cj, pi][0], betas[cj, pi][1])
            e_g = jnp.exp(gc128)
            kf = ks[cj, pi].astype(F32)
            k2 = jnp.concatenate([kf, kf], axis=1)
            qf = qs[cj, pi].astype(F32)
            q2 = jnp.concatenate([qf, qf], axis=1)
            vb = v_ref[rows, pi * 2 * hd:(pi + 1) * 2 * hd].astype(F32) * beta128
            kbg = k2 * beta128 * e_g
            rhs = jnp.concatenate([_block_diag2(vb, hd), _block_diag2(kbg, hd)], axis=1).astype(BF)
            wk = dot(t_inv.astype(BF), rhs)
            slot = slot0 + cj
            w_ref[slot, pi] = wk[:, :2 * hd]
            lhs_ref[slot, pi] = jnp.concatenate([wk[:, 2 * hd:], q2 * e_g], axis=0).astype(BF)
            ks_ref[slot, pi] = _block_diag2(k2 * jnp.exp(gl128 - gc128), hd).astype(BF)
            dec_ref[slot, pi] = jnp.broadcast_to(jnp.exp(gl128), (SUBLANES, 2 * hd))
        yield

    def recur(gi, slot0):
        ng = ng_ref[...]
        for cj in range(cpg):
            slot = slot0 + cj
            rows = chunk_rows(gi, cj)
            states = [state_ref[pi] for pi in pairs]
            ks_qs = [dot(lhs_ref[slot, pi], states[pi].astype(BF)) for pi in pairs]
            yield
            v_bds = [_block_diag2(w_ref[slot, pi] - ks_qs[pi][:ck], hd).astype(BF) for pi in pairs]
            upds = [lax.dot_general(ks_ref[slot, pi], v_bds[pi], (((0,), (0,)), ((), ())),
                                    preferred_element_type=F32) for pi in pairs]
            for pi in pairs:
                state_ref[pi] = states[pi] * dec_ref[slot, pi, 0:1, :] + upds[pi]
            yield
            outs = [ks_qs[pi][ck:] + dot(attn_ref[slot, pi], v_bds[pi]) for pi in pairs]
            for pi in pairs:
                for hh in range(2):
                    sl = slice((2 * pi + hh) * hd, (2 * pi + hh + 1) * hd)
                    oh = outs[pi][:, hh * hd:(hh + 1) * hd]
                    ms = jnp.mean(oh * oh, axis=-1, keepdims=True)
                    zh = z_ref[rows, sl].astype(F32)
                    gate = zh * jax.nn.sigmoid(zh)
                    o_ref[rows, sl] = (oh * lax.rsqrt(ms + EPS) * ng * gate).astype(o_ref.dtype)
            yield

    for _ in prepare(0, 0):
        pass

    def body(g2, carry):
        g_even = 2 * g2
        _alternate(recur(g_even, 0), prepare(g_even + 1, cpg))
        _alternate(recur(g_even + 1, cpg), prepare(g_even + 2, 0))
        return carry

    lax.fori_loop(0, n_groups // 2 - 1, body, 0)
    _alternate(recur(n_groups - 2, 0), prepare(n_groups - 1, cpg))
    for _ in recur(n_groups - 1, cpg):
        pass


def _gdn_delta(qkv, zsrc, z_col0, bg, norm_g, d_qk, d_v, ts=1024, pp=4, cpg=4):
    b, s, _ = qkv.shape
    hd = GDN_HEAD_DIM
    ck = GDN_CHUNK
    n_heads = d_v // hd
    n_pairs = n_heads // 2
    ts = min(ts, s)
    assert d_qk // hd == n_pairs and n_pairs % pp == 0 and s % ts == 0 and ts % (2 * cpg * ck) == 0
    qw = pp * hd
    vw = 2 * pp * hd
    assert d_qk % qw == 0 and (2 * d_qk) % vw == 0 and z_col0 % vw == 0
    k_blk0 = d_qk // qw
    v_blk0 = 2 * d_qk // vw
    z_blk0 = z_col0 // vw
    n_slots = 2 * cpg
    return pl.pallas_call(
        functools.partial(_gdn_delta_kernel, n_heads=n_heads, pp=pp, cpg=cpg),
        grid=(b, n_pairs // pp, s // ts),
        in_specs=[pl.BlockSpec((None, ts, qw), lambda bi, pi, si: (bi, si, pi)),
                  pl.BlockSpec((None, ts, qw), lambda bi, pi, si: (bi, si, k_blk0 + pi)),
                  pl.BlockSpec((None, ts, vw), lambda bi, pi, si: (bi, si, v_blk0 + pi)),
                  pl.BlockSpec((None, ts, vw), lambda bi, pi, si: (bi, si, z_blk0 + pi)),
                  pl.BlockSpec((None, ts, LANES), lambda bi, pi, si: (bi, si, 0)),
                  pl.BlockSpec((1, hd), lambda bi, pi, si: (0, 0))],
        out_specs=pl.BlockSpec((None, ts, vw), lambda bi, pi, si: (bi, si, pi)),
        out_shape=jax.ShapeDtypeStruct((b, s, d_v), BF),
        scratch_shapes=[pltpu.VMEM((pp, 2 * hd, 2 * hd), F32),
                        pltpu.VMEM((n_slots, pp, ck, 2 * hd), F32),
                        pltpu.VMEM((n_slots, pp, 2 * ck, 2 * hd), BF),
                        pltpu.VMEM((n_slots, pp, ck, 2 * ck), BF),
                        pltpu.VMEM((n_slots, pp, 2 * ck, 2 * hd), BF),
                        pltpu.VMEM((n_slots, pp, SUBLANES, 2 * hd), F32)],
        compiler_params=_params(3),
        name="gdn_delta",
    )(qkv, qkv, qkv, zsrc, bg, norm_g.reshape(1, hd))


def kernel(x, mem, norm_mix_g, norm_mem_g, mem_kv_w, norm_ffn_g, ffn_w_in, ffn_w_out, lru_w_in, lru_w_out, lru_conv_w, lru_conv_b, lru_gate_a_w, lru_gate_a_b, lru_gate_x_w, lru_gate_x_b, lru_lambda, gdn_w_in, gdn_w_out, gdn_conv_w, gdn_a_log, gdn_dt_bias, gdn_norm_g, final_norm_g):
    b, s, d = x.shape
    m = b * s
    mem_len = mem.shape[1]
    depth = norm_mix_g.shape[0]
    d_rnn = lru_lambda.shape[1]
    n_vh = gdn_a_log.shape[1]
    d_v = n_vh * GDN_HEAD_DIM
    d_qk = (gdn_conv_w.shape[2] - d_v) // 2
    gdn_main = 2 * d_qk + 2 * d_v
    tail_blocks = -(-(gdn_w_in.shape[2] - gdn_main) // LANES)
    gdn_w_in_t = jnp.swapaxes(gdn_w_in, 1, 2)

    h = x.reshape(m, d)
    mem2 = mem.reshape(b * mem_len, d)
    hb = None
    for i in range(depth):
        j = i // 2
        hn, gmix = (_rmsnorm(h, norm_mix_g[i], BF), None) if hb is None else (hb, norm_mix_g[i])
        mem_n = _rmsnorm(mem2, norm_mem_g[i], BF)
        kv = _matmul([mem_n], mem_kv_w, i, 0, 2 * d, BF).reshape(b, mem_len, 2 * d)
        if i % 2 == 0:
            proj = _matmul([hn], lru_w_in, j, 0, 2 * d_rnn, BF, tm=1024, gain=gmix).reshape(b, s, 2 * d_rnn)
            xq = _matmul([hn], lru_w_in, j, 2 * d_rnn, d, BF, tm=1024, gain=gmix)
            y = _rglru(proj, lru_conv_w[j], lru_conv_b[j], lru_gate_a_w[j], lru_gate_a_b[j],
                       lru_gate_x_w[j], lru_gate_x_b[j], lru_lambda[j], d_rnn)
            w_out = lru_w_out
        else:
            proj = _matmul([hn], gdn_w_in_t, j, 0, gdn_main, BF, tm=1024, w_t=True,
                           gain=gmix).reshape(b, s, gdn_main)
            w_tail = _row_window(gdn_w_in_t, j, gdn_main, tail_blocks)
            w_ba = jnp.pad(w_tail[:2 * n_vh], ((0, LANES - 2 * n_vh), (0, 0)))[None]
            ba = _matmul([hn], w_ba, 0, 0, LANES, F32, tm=1024, w_t=True, gain=gmix).reshape(b, s, LANES)
            xq = _matmul([hn], w_tail[None, 2 * n_vh:2 * n_vh + d], 0, 0, d, BF, tm=1024, w_t=True,
                         gain=gmix)
            qkv, bg = _gdn_prep(proj, ba, gdn_conv_w[j], gdn_a_log[j], gdn_dt_bias[j], d_qk, d_v)
            y = _gdn_delta(qkv, proj, 2 * d_qk + d_v, bg, gdn_norm_g[j], d_qk, d_v)
            w_out = gdn_w_out
        xo = _xattn(xq.reshape(b, s, d), kv)
        y2 = y.reshape(m, y.shape[-1])
        h, hb = _matmul([y2, xo.reshape(m, d)], w_out, j, 0, d, F32, residual=h, emit_bf16=True)
        act = _ffn_in(hb, ffn_w_in, i, norm_ffn_g[i])
        h, hb = _matmul([act], ffn_w_out, i, 0, d, F32, residual=h, emit_bf16=True)
    return _rmsnorm(h, final_norm_g, F32).reshape(b, s, d)
```

```python
import functools

import jax
import jax.numpy as jnp
from jax import lax
from jax.experimental import pallas as pl
from jax.experimental.pallas import tpu as pltpu

F32 = jnp.float32
BF = jnp.bfloat16

EPS = 1e-6
CONV_W = 4
LRU_BLOCKS = 16
LRU_C = 8.0
GDN_HEAD_DIM = 128
GDN_CHUNK = 64
X_HEADS = 4

LANES = 128
SUBLANES = 8
VMEM_LIMIT = 56 * 1024 * 1024


def _params(n_axes):
    return pltpu.CompilerParams(
        dimension_semantics=("arbitrary",) * n_axes, vmem_limit_bytes=VMEM_LIMIT)


def _rmsnorm_kernel(x_ref, g_ref, o_ref):
    x = x_ref[...]
    ms = jnp.mean(x * x, axis=-1, keepdims=True)
    o_ref[...] = (x * lax.rsqrt(ms + EPS) * g_ref[...]).astype(o_ref.dtype)


def _rmsnorm(x, g, out_dtype, tm=512):
    m, d = x.shape
    tm = min(tm, m)
    return pl.pallas_call(
        _rmsnorm_kernel,
        grid=(m // tm,),
        in_specs=[pl.BlockSpec((tm, d), lambda i: (i, 0)),
                  pl.BlockSpec((1, d), lambda i: (0, 0))],
        out_specs=pl.BlockSpec((tm, d), lambda i: (i, 0)),
        out_shape=jax.ShapeDtypeStruct((m, d), out_dtype),
        compiler_params=_params(1),
        name="rmsnorm",
    )(x, g.reshape(1, d))


_CAST_ROWS = 512


def _cast_weight(w_ref, wb_ref, g_ref=None):
    k = w_ref.shape[0]
    rows = _CAST_ROWS if k % _CAST_ROWS == 0 else k

    def body(r, carry):
        sl = pl.ds(pl.multiple_of(r * rows, rows), rows)
        blk = w_ref[sl, :]
        if g_ref is not None:
            blk = blk * g_ref[sl, :]
        wb_ref[sl, :] = blk.astype(BF)
        return carry

    lax.fori_loop(0, k // rows, body, 0)


def _cast_weight_t(wt_ref, wb_ref, g_ref=None):
    k = wt_ref.shape[1]
    step = _CAST_ROWS if k % _CAST_ROWS == 0 else k
    for r in range(k // step):
        sl = slice(r * step, (r + 1) * step)
        blk = wt_ref[:, sl].T
        if g_ref is not None:
            blk = blk * g_ref[sl, :]
        wb_ref[sl, :] = blk.astype(BF)


def _row_rms_scale(x_ref, r_ref, rows):
    @pl.when(pl.program_id(0) == 0)
    def _():
        xf = x_ref[...].astype(F32)
        r = lax.rsqrt(jnp.mean(xf * xf, axis=-1, keepdims=True) + EPS)
        r_ref[rows, :] = jnp.broadcast_to(r, (r.shape[0], LANES))


def _scale_rows(acc, r):
    reps = acc.shape[1] // LANES
    return acc * (jnp.concatenate([r] * reps, axis=1) if reps > 1 else r)


def _mm_kernel(*refs, k_sizes, has_res, w_t, norm, emit_bf16):
    n_x = len(k_sizes)
    x_refs = refs[:n_x]
    pos = n_x
    w_ref = refs[pos]
    pos += 1
    res_ref = refs[pos] if has_res else None
    pos += has_res
    g_ref = refs[pos] if norm else None
    pos += norm
    o_ref = refs[pos]
    pos += 1
    ob_ref = refs[pos] if emit_bf16 else None
    pos += emit_bf16
    wb_ref = refs[pos]
    r_ref = refs[pos + 1] if norm else None
    tm = o_ref.shape[0]
    rows = pl.ds(pl.multiple_of(pl.program_id(1) * tm, tm), tm)

    @pl.when(pl.program_id(1) == 0)
    def _():
        if w_t:
            _cast_weight_t(w_ref, wb_ref, g_ref)
        else:
            _cast_weight(w_ref, wb_ref, g_ref)

    if norm:
        _row_rms_scale(x_refs[0], r_ref, rows)

    acc = None
    off = 0
    for x_ref, k in zip(x_refs, k_sizes):
        part = jnp.dot(x_ref[...], wb_ref[off:off + k, :], preferred_element_type=F32)
        acc = part if acc is None else acc + part
        off += k
    if norm:
        acc = _scale_rows(acc, r_ref[rows, :])
    if has_res:
        acc = acc + res_ref[...]
    o_ref[...] = acc.astype(o_ref.dtype)
    if emit_bf16:
        ob_ref[...] = acc.astype(BF)


def _matmul(xs, w, layer, col0, n_cols, out_dtype, residual=None, tm=512, tn=512, w_t=False,
            gain=None, emit_bf16=False):
    m = xs[0].shape[0]
    k_sizes = tuple(x.shape[1] for x in xs)
    k_tot = sum(k_sizes)
    norm = gain is not None
    assert w.shape[2 if w_t else 1] == k_tot and not (norm and (residual is not None or len(xs) > 1))
    tm = min(tm, m)
    tn = min(tn, n_cols)
    assert m % tm == 0 and n_cols % tn == 0 and col0 % tn == 0
    cb0 = col0 // tn
    in_specs = [pl.BlockSpec((tm, k), lambda j, i: (i, 0)) for k in k_sizes]
    if w_t:
        in_specs.append(pl.BlockSpec((None, tn, k_tot), lambda j, i: (layer, j + cb0, 0)))
    else:
        in_specs.append(pl.BlockSpec((None, k_tot, tn), lambda j, i: (layer, 0, j + cb0)))
    args = list(xs) + [w]
    if residual is not None:
        in_specs.append(pl.BlockSpec((tm, tn), lambda j, i: (i, j)))
        args.append(residual)
    scratch = [pltpu.VMEM((k_tot, tn), BF)]
    if norm:
        in_specs.append(pl.BlockSpec((k_tot, 1), lambda j, i: (0, 0)))
        args.append(gain.reshape(k_tot, 1))
        scratch.append(pltpu.VMEM((m, LANES), F32))
    out_spec = pl.BlockSpec((tm, tn), lambda j, i: (i, j))
    out_shape = jax.ShapeDtypeStruct((m, n_cols), out_dtype)
    return pl.pallas_call(
        functools.partial(_mm_kernel, k_sizes=k_sizes, has_res=residual is not None, w_t=w_t,
                          norm=norm, emit_bf16=emit_bf16),
        grid=(n_cols // tn, m // tm),
        in_specs=in_specs,
        out_specs=[out_spec, out_spec] if emit_bf16 else out_spec,
        out_shape=[out_shape, jax.ShapeDtypeStruct((m, n_cols), BF)] if emit_bf16 else out_shape,
        scratch_shapes=scratch,
        compiler_params=_params(2),
        name="matmul",
    )(*args)


def _copy_kernel(w_ref, o_ref):
    o_ref[...] = w_ref[...]


def _row_window(w, layer, row0, n_blocks):
    _, _, k = w.shape
    assert row0 % LANES == 0
    rb0 = row0 // LANES
    return pl.pallas_call(
        _copy_kernel,
        grid=(n_blocks,),
        in_specs=[pl.BlockSpec((None, LANES, k), lambda j: (layer, j + rb0, 0))],
        out_specs=pl.BlockSpec((LANES, k), lambda j: (j, 0)),
        out_shape=jax.ShapeDtypeStruct((n_blocks * LANES, k), w.dtype),
        compiler_params=_params(1),
        name="row_window",
    )(w)


def _ffn_in_kernel(x_ref, wg_ref, wu_ref, g_ref, o_ref, wgb_ref, wub_ref, r_ref):
    tm = o_ref.shape[0]
    rows = pl.ds(pl.multiple_of(pl.program_id(1) * tm, tm), tm)

    @pl.when(pl.program_id(1) == 0)
    def _():
        _cast_weight(wg_ref, wgb_ref, g_ref)
        _cast_weight(wu_ref, wub_ref, g_ref)

    _row_rms_scale(x_ref, r_ref, rows)
    x = x_ref[...]
    r = r_ref[rows, :]
    gate = _scale_rows(jnp.dot(x, wgb_ref[...], preferred_element_type=F32), r)
    up = _scale_rows(jnp.dot(x, wub_ref[...], preferred_element_type=F32), r)
    o_ref[...] = (gate * jax.nn.sigmoid(gate) * up).astype(o_ref.dtype)


def _ffn_in(x, w, layer, gain, tm=1024, tn=512):
    m, k = x.shape
    f = w.shape[2] // 2
    assert m % tm == 0 and f % tn == 0
    nb = f // tn
    return pl.pallas_call(
        _ffn_in_kernel,
        grid=(nb, m // tm),
        in_specs=[pl.BlockSpec((tm, k), lambda j, i: (i, 0)),
                  pl.BlockSpec((None, k, tn), lambda j, i: (layer, 0, j)),
                  pl.BlockSpec((None, k, tn), lambda j, i: (layer, 0, j + nb)),
                  pl.BlockSpec((k, 1), lambda j, i: (0, 0))],
        out_specs=pl.BlockSpec((tm, tn), lambda j, i: (i, j)),
        out_shape=jax.ShapeDtypeStruct((m, f), BF),
        scratch_shapes=[pltpu.VMEM((k, tn), BF), pltpu.VMEM((k, tn), BF), pltpu.VMEM((m, LANES), F32)],
        compiler_params=_params(2),
        name="ffn_in",
    )(x, w, w, gain.reshape(k, 1))


def _xattn_kernel(q_ref, k_ref, v_ref, o_ref, *, heads, hd):
    scale = hd ** -0.5
    for h in range(heads):
        sl = slice(h * hd, (h + 1) * hd)
        s = lax.dot_general(q_ref[:, sl], k_ref[:, sl], (((1,), (1,)), ((), ())),
                            preferred_element_type=F32) * scale
        mx = jnp.max(s, axis=-1, keepdims=True)
        p = jnp.exp(s - mx)
        denom = jnp.sum(p, axis=-1, keepdims=True)
        o = jnp.dot(p.astype(BF), v_ref[:, sl], preferred_element_type=F32)
        o_ref[:, sl] = (o / denom).astype(o_ref.dtype)


def _xattn(q, kv, ts=512):
    b, s, d = q.shape
    mem = kv.shape[1]
    return pl.pallas_call(
        functools.partial(_xattn_kernel, heads=X_HEADS, hd=d // X_HEADS),
        grid=(b, s // ts),
        in_specs=[pl.BlockSpec((None, ts, d), lambda bi, si: (bi, si, 0)),
                  pl.BlockSpec((None, mem, d), lambda bi, si: (bi, 0, 0)),
                  pl.BlockSpec((None, mem, d), lambda bi, si: (bi, 0, 1))],
        out_specs=pl.BlockSpec((None, ts, d), lambda bi, si: (bi, si, 0)),
        out_shape=jax.ShapeDtypeStruct((b, s, d), BF),
        compiler_params=_params(2),
        name="xattn",
    )(q, kv, kv)


def _causal_conv(x, tail, cw_ref, cols):
    t = x.shape[0]
    row8 = lax.broadcasted_iota(jnp.int32, (SUBLANES, x.shape[1]), 0)
    acc = x * cw_ref[CONV_W - 1:CONV_W, cols]
    for d in range(1, CONV_W):
        xr = pltpu.roll(x, d, axis=0)
        head = jnp.where(row8 < d, pltpu.roll(tail, d, axis=0), xr[0:SUBLANES])
        xs = jnp.concatenate([head, xr[SUBLANES:t]], axis=0)
        acc = acc + xs * cw_ref[CONV_W - 1 - d:CONV_W - d, cols]
    return acc


def _band_layout(c, tn, blk):
    starts, need = [], 0
    for j in range(c // tn):
        lo = (j * tn // blk) * blk
        hi = (((j + 1) * tn - 1) // blk + 1) * blk
        st = (lo // LANES) * LANES
        starts.append(st)
        need = max(need, hi - st)
    kw = -(-need // LANES) * LANES
    starts = [min(st, c - kw) for st in starts]
    return tuple(starts), kw


def _band_weights(gw, starts, kw, tn):
    dense = jax.scipy.linalg.block_diag(*[gw[i] for i in range(gw.shape[0])])
    return jnp.stack([dense[st:st + kw, j * tn:(j + 1) * tn]
                      for j, st in enumerate(starts)]).astype(BF)


def _scan_linear(a, b, h0):
    t = a.shape[0]
    row = lax.broadcasted_iota(jnp.int32, (SUBLANES, a.shape[1]), 0)
    out = []
    carry = h0
    for g in range(t // SUBLANES):
        ag = a[g * SUBLANES:(g + 1) * SUBLANES]
        bg = b[g * SUBLANES:(g + 1) * SUBLANES]
        d = 1
        while d < SUBLANES:
            keep = row >= d
            a_s = jnp.where(keep, pltpu.roll(ag, d, axis=0), 1.0)
            b_s = jnp.where(keep, pltpu.roll(bg, d, axis=0), 0.0)
            bg = ag * b_s + bg
            ag = ag * a_s
            d *= 2
        hg = bg + ag * carry
        carry = hg[SUBLANES - 1:SUBLANES]
        out.append(hg)
    return jnp.concatenate(out, axis=0)


def _rglru_kernel(xb_ref, gb_ref, cw_ref, cb_ref, wa_ref, wx_ref, ba_ref, bx_ref, lam_ref,
                  y_ref, tail_ref, hc_ref, xc_ref, xcb_ref, *, starts, kw, tn):
    t, c = xb_ref.shape

    @pl.when(pl.program_id(1) == 0)
    def _():
        tail_ref[...] = jnp.zeros_like(tail_ref)
        hc_ref[...] = jnp.zeros_like(hc_ref)

    for j in range(c // tn):
        cols = slice(j * tn, (j + 1) * tn)
        x = xb_ref[:, cols].astype(F32)
        xc = _causal_conv(x, tail_ref[:, cols], cw_ref, cols) + cb_ref[:, cols]
        tail_ref[:, cols] = x[t - SUBLANES:t]
        xc_ref[:, cols] = xc
        xcb_ref[:, cols] = xc.astype(BF)

    for j, st in enumerate(starts):
        cols = slice(j * tn, (j + 1) * tn)
        xw = xcb_ref[:, st:st + kw]
        r = jax.nn.sigmoid(jnp.dot(xw, wa_ref[j], preferred_element_type=F32) + ba_ref[:, cols])
        i = jax.nn.sigmoid(jnp.dot(xw, wx_ref[j], preferred_element_type=F32) + bx_ref[:, cols])
        neg_lam = -lam_ref[:, cols]
        softplus = jnp.maximum(neg_lam, 0.0) + jnp.log1p(jnp.exp(-jnp.abs(neg_lam)))
        log_a = (-LRU_C) * r * softplus
        a = jnp.exp(log_a)
        th = jnp.tanh(log_a)
        b_in = jnp.sqrt(-2.0 * th / (1.0 - th)) * (i * xc_ref[:, cols])
        h = _scan_linear(a, b_in, hc_ref[0:1, cols])
        hc_ref[:, cols] = jnp.broadcast_to(h[t - 1:t], (SUBLANES, tn))
        y_ref[:, cols] = (h * jax.nn.gelu(gb_ref[:, cols].astype(F32))).astype(y_ref.dtype)


def _rglru(proj, conv_w, conv_b, gate_a_w, gate_a_b, gate_x_w, gate_x_b, lam, c, t=128, tn=256):
    b, s, _ = proj.shape
    starts, kw = _band_layout(c, tn, c // LRU_BLOCKS)
    wa = _band_weights(gate_a_w, starts, kw, tn)
    wx = _band_weights(gate_x_w, starts, kw, tn)
    nt = c // tn
    row = lambda v: v.reshape(1, c)
    full = lambda shape: pl.BlockSpec(shape, lambda bi, ti: (0,) * len(shape))
    return pl.pallas_call(
        functools.partial(_rglru_kernel, starts=starts, kw=kw, tn=tn),
        grid=(b, s // t),
        in_specs=[pl.BlockSpec((None, t, c), lambda bi, ti: (bi, ti, 0)),
                  pl.BlockSpec((None, t, c), lambda bi, ti: (bi, ti, 1)),
                  full((CONV_W, c)), full((1, c)),
                  full((nt, kw, tn)), full((nt, kw, tn)),
                  full((1, c)), full((1, c)), full((1, c))],
        out_specs=pl.BlockSpec((None, t, c), lambda bi, ti: (bi, ti, 0)),
        out_shape=jax.ShapeDtypeStruct((b, s, c), BF),
        scratch_shapes=[pltpu.VMEM((SUBLANES, c), F32), pltpu.VMEM((SUBLANES, c), F32),
                        pltpu.VMEM((t, c), F32), pltpu.VMEM((t, c), BF)],
        compiler_params=_params(2),
        name="rglru",
    )(proj, proj, conv_w, row(conv_b), wa, wx, row(gate_a_b), row(gate_x_b), row(lam))


def _gdn_prep_kernel(x_ref, ba_ref, cw_ref, alog_ref, dtb_ref, o_ref, bg_ref, tail_ref,
                     *, d_qk, n_heads, tc):
    t, c = x_ref.shape

    @pl.when(pl.program_id(1) == 0)
    def _():
        tail_ref[...] = jnp.zeros_like(tail_ref)

    hd = GDN_HEAD_DIM
    for j in range(c // tc):
        cols = slice(j * tc, (j + 1) * tc)
        x = x_ref[:, cols].astype(F32)
        y = _causal_conv(x, tail_ref[:, cols], cw_ref, cols)
        tail_ref[:, cols] = x[t - SUBLANES:t]
        y = y * jax.nn.sigmoid(y)
        if j * tc < 2 * d_qk:
            scale = hd ** -0.5 if j * tc < d_qk else 1.0
            for hh in range(tc // hd):
                yh = y[:, hh * hd:(hh + 1) * hd]
                ss = jnp.sum(yh * yh, axis=-1, keepdims=True)
                yn = yh * lax.rsqrt(ss + EPS)
                if scale != 1.0:
                    yn = yn * scale
                o_ref[:, j * tc + hh * hd:j * tc + (hh + 1) * hd] = yn.astype(o_ref.dtype)
        else:
            o_ref[:, cols] = y.astype(o_ref.dtype)

    ba = ba_ref[...]
    lane = lax.broadcasted_iota(jnp.int32, ba.shape, 1)
    z = ba + dtb_ref[...]
    softplus = jnp.maximum(z, 0.0) + jnp.log1p(jnp.exp(-jnp.abs(z)))
    g = -jnp.exp(alog_ref[...]) * softplus
    bg_ref[...] = jnp.where(lane < n_heads, jax.nn.sigmoid(ba), g)


def _gdn_prep(proj, ba, conv_w, a_log, dt_bias, d_qk, d_v, t=256, tc=512):
    b, s, _ = proj.shape
    c = 2 * d_qk + d_v
    nh = a_log.shape[0]
    pad = lambda v: jnp.zeros((1, LANES), F32).at[0, nh:2 * nh].set(v)
    full = lambda shape: pl.BlockSpec(shape, lambda bi, ti: (0,) * len(shape))
    return pl.pallas_call(
        functools.partial(_gdn_prep_kernel, d_qk=d_qk, n_heads=nh, tc=tc),
        grid=(b, s // t),
        in_specs=[pl.BlockSpec((None, t, c), lambda bi, ti: (bi, ti, 0)),
                  pl.BlockSpec((None, t, LANES), lambda bi, ti: (bi, ti, 0)),
                  full((CONV_W, c)), full((1, LANES)), full((1, LANES))],
        out_specs=[pl.BlockSpec((None, t, c), lambda bi, ti: (bi, ti, 0)),
                   pl.BlockSpec((None, t, LANES), lambda bi, ti: (bi, ti, 0))],
        out_shape=[jax.ShapeDtypeStruct((b, s, c), BF),
                   jax.ShapeDtypeStruct((b, s, LANES), F32)],
        scratch_shapes=[pltpu.VMEM((SUBLANES, c), F32)],
        compiler_params=_params(2),
        name="gdn_prep",
    )(proj, ba, conv_w, pad(a_log), pad(dt_bias))


def _split2(x):
    hi = x.astype(BF)
    mid = (x - hi.astype(F32)).astype(BF)
    return hi, mid


def _block_diag2(x, half):
    lane = lax.broadcasted_iota(jnp.int32, x.shape, 1)
    zero = jnp.zeros_like(x)
    return jnp.concatenate([jnp.where(lane < half, x, zero), jnp.where(lane >= half, x, zero)], axis=0)


def _alternate(first, second):
    live = [first, second]
    while live:
        for gen in list(live):
            if next(gen, "done") == "done":
                live.remove(gen)


def _gdn_delta_kernel(q_ref, k_ref, v_ref, z_ref, bg_ref, ng_ref, o_ref,
                      state_ref, w_ref, lhs_ref, attn_ref, ks_ref, dec_ref, *, n_heads, pp, cpg):
    ck = GDN_CHUNK
    hd = GDN_HEAD_DIM
    ts = q_ref.shape[0]
    n_groups = ts // (ck * cpg)
    pair0 = pl.program_id(1) * pp

    @pl.when(pl.program_id(2) == 0)
    def _():
        state_ref[...] = jnp.zeros_like(state_ref)

    lane64 = lax.broadcasted_iota(jnp.int32, (ck, 2 * ck), 1)
    row64 = lax.broadcasted_iota(jnp.int32, (ck, 2 * ck), 0)
    col_in_head = jnp.where(lane64 < ck, lane64, lane64 - ck)
    first64 = lane64 < ck
    causal = row64 >= col_in_head
    strict = row64 > col_in_head
    eye = row64 == col_in_head
    eye_f = jnp.where(eye, 1.0, 0.0)
    mask_lo = jnp.where(first64, 1.0, 0.0).astype(BF)
    mask_hi = jnp.where(first64, 0.0, 1.0).astype(BF)
    lane128 = lax.broadcasted_iota(jnp.int32, (ck, 2 * hd), 1)
    first128 = lane128 < hd
    lane_bg = lax.broadcasted_iota(jnp.int32, (ck, LANES), 1)
    dot = functools.partial(jnp.dot, preferred_element_type=F32)
    pairs = range(pp)
    insts = [(cj, pi) for cj in range(cpg) for pi in pairs]
    n_sq = 1
    while (2 << n_sq) < ck:
        n_sq += 1

    def chunk_rows(gi, cj):
        return pl.ds(pl.multiple_of((gi * cpg + cj) * ck, ck), ck)

    def bd_bf(x):
        return jnp.concatenate([x * mask_lo, x * mask_hi], axis=0)

    def precise_rhs(m_hi, m_mid):
        bd_hi = bd_bf(m_hi)
        return jnp.concatenate([bd_hi, bd_hi, bd_bf(m_mid)], axis=0)

    def precise_lhs(x_hi, x_mid):
        return jnp.concatenate([x_hi, x_mid, x_hi], axis=1)

    def prepare(gi, slot0):
        betas, gcs, qs, ks, a_mats = {}, {}, {}, {}, {}
        for cj, pi in insts:
            rows = chunk_rows(gi, cj)
            bg = bg_ref[rows, :]

            def column(l, bg=bg):
                return jnp.sum(jnp.where(lane_bg == l, bg, 0.0), axis=-1, keepdims=True)

            h0 = 2 * (pair0 + pi)
            beta0, beta1 = column(h0), column(h0 + 1)
            gc = jnp.where(first64, column(n_heads + h0), column(n_heads + h0 + 1))
            d = 1
            while d < ck:
                if d < SUBLANES:
                    gc = gc + jnp.where(row64 >= d, pltpu.roll(gc, d, axis=0), 0.0)
                else:
                    gc = jnp.concatenate([gc[:d], gc[d:] + gc[:ck - d]], axis=0)
                d *= 2
            g_row = jnp.sum(jnp.where(eye, gc, 0.0), axis=0, keepdims=True)
            decay = jnp.where(causal, jnp.exp(jnp.where(causal, gc - g_row, 0.0)), 0.0)

            q = q_ref[rows, pi * hd:(pi + 1) * hd]
            k = k_ref[rows, pi * hd:(pi + 1) * hd]
            qk = jnp.concatenate([q, k], axis=0)
            kk = jnp.concatenate([k, k], axis=0)
            prod = lax.dot_general(qk, kk, (((1,), (1,)), ((), ())), preferred_element_type=F32)
            beta64 = jnp.where(first64, beta0, beta1)
            a_mats[cj, pi] = jnp.where(strict, prod[ck:] * beta64 * decay, 0.0)
            attn_ref[slot0 + cj, pi] = jnp.where(causal, prod[:ck] * decay, 0.0).astype(BF)
            betas[cj, pi] = (beta0, beta1)
            gcs[cj, pi] = gc
            qs[cj, pi] = q
            ks[cj, pi] = k
        yield

        p_mats, m_mats, a_split = {}, {}, {}
        for key in insts:
            a_split[key] = _split2(a_mats[key])
            a_hi = a_split[key][0]
            m_mats[key] = dot(a_hi, bd_bf(a_hi))
            p_mats[key] = eye_f - a_mats[key]
        yield
        for step in range(n_sq):
            for key in insts:
                m_bf = m_mats[key].astype(BF)
                p_bf = p_mats[key].astype(BF)
                if step + 1 < n_sq:
                    both = dot(jnp.concatenate([p_bf, m_bf], axis=0), bd_bf(m_bf))
                    p_mats[key] = p_mats[key] + both[:ck]
                    m_mats[key] = both[ck:]
                else:
                    p_mats[key] = p_mats[key] + dot(p_bf, bd_bf(m_bf))
            yield
        t_his, resid = {}, {}
        for key in insts:
            t_hi, t_mid = _split2(p_mats[key])
            a_hi, a_mid = a_split[key]
            a_t = dot(precise_lhs(a_hi, a_mid), precise_rhs(t_hi, t_mid))
            resid[key] = (eye_f - p_mats[key]) - a_t
            t_his[key] = t_hi
        yield
        for key in insts:
            p_mats[key] = p_mats[key] + dot(t_his[key], bd_bf(resid[key].astype(BF)))
        yield

        for cj, pi in insts:
            rows = chunk_rows(gi, cj)
            t_inv = p_mats[cj, pi]
            gc = gcs[cj, pi]
            gc128 = jnp.where(first128, gc[:, 0:1], gc[:, ck:ck + 1])
            gl128 = gc128[ck - 1:ck, :]
            beta128 = jnp.where(first128, betas[cj, pi][0], betas[cj, pi][1])
            e_g = jnp.exp(gc128)
            kf = ks[cj, pi].astype(F32)
            k2 = jnp.concatenate([kf, kf], axis=1)
            qf = qs[cj, pi].astype(F32)
            q2 = jnp.concatenate([qf, qf], axis=1)
            vb = v_ref[rows, pi * 2 * hd:(pi + 1) * 2 * hd].astype(F32) * beta128
            kbg = k2 * beta128 * e_g
            rhs = jnp.concatenate([_block_diag2(vb, hd), _block_diag2(kbg, hd)], axis=1).astype(BF)
            wk = dot(t_inv.astype(BF), rhs)
            slot = slot0 + cj
            w_ref[slot, pi] = wk[:, :2 * hd]
            lhs_ref[slot, pi] = jnp.concatenate([wk[:, 2 * hd:], q2 * e_g], axis=0).astype(BF)
            ks_ref[slot, pi] = _block_diag2(k2 * jnp.exp(gl128 - gc128), hd).astype(BF)
            dec_ref[slot, pi] = jnp.broadcast_to(jnp.exp(gl128), (SUBLANES, 2 * hd))
        yield

    def recur(gi, slot0):
        ng = ng_ref[...]
        for cj in range(cpg):
            slot = slot0 + cj
            rows = chunk_rows(gi, cj)
            states = [state_ref[pi] for pi in pairs]
            ks_qs = [dot(lhs_ref[slot, pi], states[pi].astype(BF)) for pi in pairs]
            yield
            v_bds = [_block_diag2(w_ref[slot, pi] - ks_qs[pi][:ck], hd).astype(BF) for pi in pairs]
            upds = [lax.dot_general(ks_ref[slot, pi], v_bds[pi], (((0,), (0,)), ((), ())),
                                    preferred_element_type=F32) for pi in pairs]
            for pi in pairs:
                state_ref[pi] = states[pi] * dec_ref[slot, pi, 0:1, :] + upds[pi]
            yield
            outs = [ks_qs[pi][ck:] + dot(attn_ref[slot, pi], v_bds[pi]) for pi in pairs]
            for pi in pairs:
                for hh in range(2):
                    sl = slice((2 * pi + hh) * hd, (2 * pi + hh + 1) * hd)
                    oh = outs[pi][:, hh * hd:(hh + 1) * hd]
                    ms = jnp.mean(oh * oh, axis=-1, keepdims=True)
                    zh = z_ref[rows, sl].astype(F32)
                    gate = zh * jax.nn.sigmoid(zh)
                    o_ref[rows, sl] = (oh * lax.rsqrt(ms + EPS) * ng * gate).astype(o_ref.dtype)
            yield

    for _ in prepare(0, 0):
        pass

    def body(g2, carry):
        g_even = 2 * g2
        _alternate(recur(g_even, 0), prepare(g_even + 1, cpg))
        _alternate(recur(g_even + 1, cpg), prepare(g_even + 2, 0))
        return carry

    lax.fori_loop(0, n_groups // 2 - 1, body, 0)
    _alternate(recur(n_groups - 2, 0), prepare(n_groups - 1, cpg))
    for _ in recur(n_groups - 1, cpg):
        pass


def _gdn_delta(qkv, zsrc, z_col0, bg, norm_g, d_qk, d_v, ts=1024, pp=4, cpg=4):
    b, s, _ = qkv.shape
    hd = GDN_HEAD_DIM
    ck = GDN_CHUNK
    n_heads = d_v // hd
    n_pairs = n_heads // 2
    ts = min(ts, s)
    assert d_qk // hd == n_pairs and n_pairs % pp == 0 and s % ts == 0 and ts % (2 * cpg * ck) == 0
    qw = pp * hd
    vw = 2 * pp * hd
    assert d_qk % qw == 0 and (2 * d_qk) % vw == 0 and z_col0 % vw == 0
    k_blk0 = d_qk // qw
    v_blk0 = 2 * d_qk // vw
    z_blk0 = z_col0 // vw
    n_slots = 2 * cpg
    return pl.pallas_call(
        functools.partial(_gdn_delta_kernel, n_heads=n_heads, pp=pp, cpg=cpg),
        grid=(b, n_pairs // pp, s // ts),
        in_specs=[pl.BlockSpec((None, ts, qw), lambda bi, pi, si: (bi, si, pi)),
                  pl.BlockSpec((None, ts, qw), lambda bi, pi, si: (bi, si, k_blk0 + pi)),
                  pl.BlockSpec((None, ts, vw), lambda bi, pi, si: (bi, si, v_blk0 + pi)),
                  pl.BlockSpec((None, ts, vw), lambda bi, pi, si: (bi, si, z_blk0 + pi)),
                  pl.BlockSpec((None, ts, LANES), lambda bi, pi, si: (bi, si, 0)),
                  pl.BlockSpec((1, hd), lambda bi, pi, si: (0, 0))],
        out_specs=pl.BlockSpec((None, ts, vw), lambda bi, pi, si: (bi, si, pi)),
        out_shape=jax.ShapeDtypeStruct((b, s, d_v), BF),
        scratch_shapes=[pltpu.VMEM((pp, 2 * hd, 2 * hd), F32),
                        pltpu.VMEM((n_slots, pp, ck, 2 * hd), F32),
                        pltpu.VMEM((n_slots, pp, 2 * ck, 2 * hd), BF),
                        pltpu.VMEM((n_slots, pp, ck, 2 * ck), BF),
                        pltpu.VMEM((n_slots, pp, 2 * ck, 2 * hd), BF),
                        pltpu.VMEM((n_slots, pp, SUBLANES, 2 * hd), F32)],
        compiler_params=_params(3),
        name="gdn_delta",
    )(qkv, qkv, qkv, zsrc, bg, norm_g.reshape(1, hd))


def kernel(x, mem, norm_mix_g, norm_mem_g, mem_kv_w, norm_ffn_g, ffn_w_in, ffn_w_out, lru_w_in, lru_w_out, lru_conv_w, lru_conv_b, lru_gate_a_w, lru_gate_a_b, lru_gate_x_w, lru_gate_x_b, lru_lambda, gdn_w_in, gdn_w_out, gdn_conv_w, gdn_a_log, gdn_dt_bias, gdn_norm_g, final_norm_g):
    b, s, d = x.shape
    m = b * s
    mem_len = mem.shape[1]
    depth = norm_mix_g.shape[0]
    d_rnn = lru_lambda.shape[1]
    n_vh = gdn_a_log.shape[1]
    d_v = n_vh * GDN_HEAD_DIM
    d_qk = (gdn_conv_w.shape[2] - d_v) // 2
    gdn_main = 2 * d_qk + 2 * d_v
    tail_blocks = -(-(gdn_w_in.shape[2] - gdn_main) // LANES)
    gdn_w_in_t = jnp.swapaxes(gdn_w_in, 1, 2)

    h = x.reshape(m, d)
    mem2 = mem.reshape(b * mem_len, d)
    hb = None
    for i in range(depth):
        j = i // 2
        hn, gmix = (_rmsnorm(h, norm_mix_g[i], BF), None) if hb is None else (hb, norm_mix_g[i])
        mem_n = _rmsnorm(mem2, norm_mem_g[i], BF)
        kv = _matmul([mem_n], mem_kv_w, i, 0, 2 * d, BF).reshape(b, mem_len, 2 * d)
        if i % 2 == 0:
            proj = _matmul([hn], lru_w_in, j, 0, 2 * d_rnn, BF, tm=1024, gain=gmix).reshape(b, s, 2 * d_rnn)
            xq = _matmul([hn], lru_w_in, j, 2 * d_rnn, d, BF, tm=1024, gain=gmix)
            y = _rglru(proj, lru_conv_w[j], lru_conv_b[j], lru_gate_a_w[j], lru_gate_a_b[j],
                       lru_gate_x_w[j], lru_gate_x_b[j], lru_lambda[j], d_rnn)
            w_out = lru_w_out
        else:
            proj_args = dict(tm=1024, w_t=True, gain=gmix)
            proj = _matmul([hn], gdn_w_in_t, j, 0, gdn_main, BF, **proj_args).reshape(b, s, gdn_main)
            w_tail = _row_window(gdn_w_in_t, j, gdn_main, tail_blocks)
            w_ba = jnp.pad(w_tail[:2 * n_vh], ((0, LANES - 2 * n_vh), (0, 0)))[None]
            ba = _matmul([hn], w_ba, 0, 0, LANES, F32, **proj_args).reshape(b, s, LANES)
            xq = _matmul([hn], w_tail[None, 2 * n_vh:2 * n_vh + d], 0, 0, d, BF, **proj_args)
            qkv, bg = _gdn_prep(proj, ba, gdn_conv_w[j], gdn_a_log[j], gdn_dt_bias[j], d_qk, d_v)
            y = _gdn_delta(qkv, proj, 2 * d_qk + d_v, bg, gdn_norm_g[j], d_qk, d_v)
            w_out = gdn_w_out
        xo = _xattn(xq.reshape(b, s, d), kv)
        y2 = y.reshape(m, y.shape[-1])
        h, hb = _matmul([y2, xo.reshape(m, d)], w_out, j, 0, d, F32, residual=h, emit_bf16=True)
        act = _ffn_in(hb, ffn_w_in, i, norm_ffn_g[i])
        last = i + 1 == depth
        out = _matmul([act], ffn_w_out, i, 0, d, F32, residual=h, emit_bf16=not last)
        h, hb = (out, None) if last else out
    return _rmsnorm(h, final_norm_g, F32).reshape(b, s, d)
```

```python
import functools

import jax
import jax.numpy as jnp
from jax import lax
from jax.experimental import pallas as pl
from jax.experimental.pallas import tpu as pltpu

F32 = jnp.float32
BF = jnp.bfloat16

EPS = 1e-6
CONV_W = 4
LRU_BLOCKS = 16
LRU_C = 8.0
GDN_HEAD_DIM = 128
GDN_CHUNK = 64
X_HEADS = 4

LANES = 128
SUBLANES = 8
VMEM_LIMIT = 56 * 1024 * 1024


def _params(n_axes):
    return pltpu.CompilerParams(
        dimension_semantics=("arbitrary",) * n_axes, vmem_limit_bytes=VMEM_LIMIT)


def _rmsnorm_kernel(x_ref, g_ref, o_ref):
    x = x_ref[...]
    ms = jnp.mean(x * x, axis=-1, keepdims=True)
    o_ref[...] = (x * lax.rsqrt(ms + EPS) * g_ref[...]).astype(o_ref.dtype)


def _rmsnorm(x, g, out_dtype, tm=512):
    m, d = x.shape
    tm = min(tm, m)
    return pl.pallas_call(
        _rmsnorm_kernel,
        grid=(m // tm,),
        in_specs=[pl.BlockSpec((tm, d), lambda i: (i, 0)),
                  pl.BlockSpec((1, d), lambda i: (0, 0))],
        out_specs=pl.BlockSpec((tm, d), lambda i: (i, 0)),
        out_shape=jax.ShapeDtypeStruct((m, d), out_dtype),
        compiler_params=_params(1),
        name="rmsnorm",
    )(x, g.reshape(1, d))


_CAST_ROWS = 512


def _cast_weight(w_ref, wb_ref, g_ref=None):
    k = w_ref.shape[0]
    rows = _CAST_ROWS if k % _CAST_ROWS == 0 else k

    def body(r, carry):
        sl = pl.ds(pl.multiple_of(r * rows, rows), rows)
        blk = w_ref[sl, :]
        if g_ref is not None:
            blk = blk * g_ref[sl, :]
        wb_ref[sl, :] = blk.astype(BF)
        return carry

    lax.fori_loop(0, k // rows, body, 0)


def _cast_weight_t(wt_ref, wb_ref, g_ref=None):
    k = wt_ref.shape[1]
    step = _CAST_ROWS if k % _CAST_ROWS == 0 else k
    for r in range(k // step):
        sl = slice(r * step, (r + 1) * step)
        blk = wt_ref[:, sl].T
        if g_ref is not None:
            blk = blk * g_ref[sl, :]
        wb_ref[sl, :] = blk.astype(BF)


def _row_rms_scale(x_ref, r_ref, rows):
    @pl.when(pl.program_id(0) == 0)
    def _():
        xf = x_ref[...].astype(F32)
        r = lax.rsqrt(jnp.mean(xf * xf, axis=-1, keepdims=True) + EPS)
        r_ref[rows, :] = jnp.broadcast_to(r, (r.shape[0], LANES))


def _scale_rows(acc, r):
    reps = acc.shape[1] // LANES
    return acc * (jnp.concatenate([r] * reps, axis=1) if reps > 1 else r)


def _mm_kernel(*refs, k_sizes, has_res, w_t, norm, emit_bf16):
    n_x = len(k_sizes)
    x_refs = refs[:n_x]
    pos = n_x
    w_ref = refs[pos]
    pos += 1
    res_ref = refs[pos] if has_res else None
    pos += has_res
    g_ref = refs[pos] if norm else None
    pos += norm
    o_ref = refs[pos]
    pos += 1
    ob_ref = refs[pos] if emit_bf16 else None
    pos += emit_bf16
    wb_ref = refs[pos]
    r_ref = refs[pos + 1] if norm else None
    tm = o_ref.shape[0]
    rows = pl.ds(pl.multiple_of(pl.program_id(1) * tm, tm), tm)

    @pl.when(pl.program_id(1) == 0)
    def _():
        if w_t:
            _cast_weight_t(w_ref, wb_ref, g_ref)
        else:
            _cast_weight(w_ref, wb_ref, g_ref)

    if norm:
        _row_rms_scale(x_refs[0], r_ref, rows)

    acc = None
    off = 0
    for x_ref, k in zip(x_refs, k_sizes):
        part = jnp.dot(x_ref[...], wb_ref[off:off + k, :], preferred_element_type=F32)
        acc = part if acc is None else acc + part
        off += k
    if norm:
        acc = _scale_rows(acc, r_ref[rows, :])
    if has_res:
        acc = acc + res_ref[...]
    o_ref[...] = acc.astype(o_ref.dtype)
    if emit_bf16:
        ob_ref[...] = acc.astype(BF)


def _matmul(xs, w, layer, col0, n_cols, out_dtype, residual=None, tm=512, tn=512, w_t=False,
            gain=None, emit_bf16=False):
    m = xs[0].shape[0]
    k_sizes = tuple(x.shape[1] for x in xs)
    k_tot = sum(k_sizes)
    norm = gain is not None
    assert w.shape[2 if w_t else 1] == k_tot and not (norm and (residual is not None or len(xs) > 1))
    tm = min(tm, m)
    tn = min(tn, n_cols)
    assert m % tm == 0 and n_cols % tn == 0 and col0 % tn == 0
    cb0 = col0 // tn
    in_specs = [pl.BlockSpec((tm, k), lambda j, i: (i, 0)) for k in k_sizes]
    if w_t:
        in_specs.append(pl.BlockSpec((None, tn, k_tot), lambda j, i: (layer, j + cb0, 0)))
    else:
        in_specs.append(pl.BlockSpec((None, k_tot, tn), lambda j, i: (layer, 0, j + cb0)))
    args = list(xs) + [w]
    if residual is not None:
        in_specs.append(pl.BlockSpec((tm, tn), lambda j, i: (i, j)))
        args.append(residual)
    scratch = [pltpu.VMEM((k_tot, tn), BF)]
    if norm:
        in_specs.append(pl.BlockSpec((k_tot, 1), lambda j, i: (0, 0)))
        args.append(gain.reshape(k_tot, 1))
        scratch.append(pltpu.VMEM((m, LANES), F32))
    out_spec = pl.BlockSpec((tm, tn), lambda j, i: (i, j))
    out_shape = jax.ShapeDtypeStruct((m, n_cols), out_dtype)
    return pl.pallas_call(
        functools.partial(_mm_kernel, k_sizes=k_sizes, has_res=residual is not None, w_t=w_t,
                          norm=norm, emit_bf16=emit_bf16),
        grid=(n_cols // tn, m // tm),
        in_specs=in_specs,
        out_specs=[out_spec, out_spec] if emit_bf16 else out_spec,
        out_shape=[out_shape, jax.ShapeDtypeStruct((m, n_cols), BF)] if emit_bf16 else out_shape,
        scratch_shapes=scratch,
        compiler_params=_params(2),
        name="matmul",
    )(*args)


def _copy_kernel(w_ref, o_ref):
    o_ref[...] = w_ref[...]


def _row_window(w, layer, row0, rows):
    _, n, k = w.shape
    assert row0 % rows == 0 and (n - row0) % rows == 0 and rows % SUBLANES == 0
    rb0 = row0 // rows
    return pl.pallas_call(
        _copy_kernel,
        grid=((n - row0) // rows,),
        in_specs=[pl.BlockSpec((None, rows, k), lambda j: (layer, j + rb0, 0))],
        out_specs=pl.BlockSpec((rows, k), lambda j: (j, 0)),
        out_shape=jax.ShapeDtypeStruct((n - row0, k), w.dtype),
        compiler_params=_params(1),
        name="row_window",
    )(w)


def _ffn_in_kernel(x_ref, wg_ref, wu_ref, g_ref, o_ref, wgb_ref, wub_ref, r_ref):
    tm = o_ref.shape[0]
    rows = pl.ds(pl.multiple_of(pl.program_id(1) * tm, tm), tm)

    @pl.when(pl.program_id(1) == 0)
    def _():
        _cast_weight(wg_ref, wgb_ref, g_ref)
        _cast_weight(wu_ref, wub_ref, g_ref)

    _row_rms_scale(x_ref, r_ref, rows)
    x = x_ref[...]
    r = r_ref[rows, :]
    gate = _scale_rows(jnp.dot(x, wgb_ref[...], preferred_element_type=F32), r)
    up = _scale_rows(jnp.dot(x, wub_ref[...], preferred_element_type=F32), r)
    o_ref[...] = (gate * jax.nn.sigmoid(gate) * up).astype(o_ref.dtype)


def _ffn_in(x, w, layer, gain, tm=1024, tn=512):
    m, k = x.shape
    f = w.shape[2] // 2
    assert m % tm == 0 and f % tn == 0
    nb = f // tn
    return pl.pallas_call(
        _ffn_in_kernel,
        grid=(nb, m // tm),
        in_specs=[pl.BlockSpec((tm, k), lambda j, i: (i, 0)),
                  pl.BlockSpec((None, k, tn), lambda j, i: (layer, 0, j)),
                  pl.BlockSpec((None, k, tn), lambda j, i: (layer, 0, j + nb)),
                  pl.BlockSpec((k, 1), lambda j, i: (0, 0))],
        out_specs=pl.BlockSpec((tm, tn), lambda j, i: (i, j)),
        out_shape=jax.ShapeDtypeStruct((m, f), BF),
        scratch_shapes=[pltpu.VMEM((k, tn), BF), pltpu.VMEM((k, tn), BF), pltpu.VMEM((m, LANES), F32)],
        compiler_params=_params(2),
        name="ffn_in",
    )(x, w, w, gain.reshape(k, 1))


def _xattn_kernel(q_ref, k_ref, v_ref, o_ref, *, heads, hd):
    scale = hd ** -0.5
    for h in range(heads):
        sl = slice(h * hd, (h + 1) * hd)
        s = lax.dot_general(q_ref[:, sl], k_ref[:, sl], (((1,), (1,)), ((), ())),
                            preferred_element_type=F32) * scale
        mx = jnp.max(s, axis=-1, keepdims=True)
        p = jnp.exp(s - mx)
        denom = jnp.sum(p, axis=-1, keepdims=True)
        o = jnp.dot(p.astype(BF), v_ref[:, sl], preferred_element_type=F32)
        o_ref[:, sl] = (o / denom).astype(o_ref.dtype)


def _xattn(q, kv, ts=512):
    b, s, d = q.shape
    mem = kv.shape[1]
    return pl.pallas_call(
        functools.partial(_xattn_kernel, heads=X_HEADS, hd=d // X_HEADS),
        grid=(b, s // ts),
        in_specs=[pl.BlockSpec((None, ts, d), lambda bi, si: (bi, si, 0)),
                  pl.BlockSpec((None, mem, d), lambda bi, si: (bi, 0, 0)),
                  pl.BlockSpec((None, mem, d), lambda bi, si: (bi, 0, 1))],
        out_specs=pl.BlockSpec((None, ts, d), lambda bi, si: (bi, si, 0)),
        out_shape=jax.ShapeDtypeStruct((b, s, d), BF),
        compiler_params=_params(2),
        name="xattn",
    )(q, kv, kv)


def _causal_conv(x, tail, cw_ref, cols):
    t = x.shape[0]
    row8 = lax.broadcasted_iota(jnp.int32, (SUBLANES, x.shape[1]), 0)
    acc = x * cw_ref[CONV_W - 1:CONV_W, cols]
    for d in range(1, CONV_W):
        xr = pltpu.roll(x, d, axis=0)
        head = jnp.where(row8 < d, pltpu.roll(tail, d, axis=0), xr[0:SUBLANES])
        xs = jnp.concatenate([head, xr[SUBLANES:t]], axis=0)
        acc = acc + xs * cw_ref[CONV_W - 1 - d:CONV_W - d, cols]
    return acc


def _band_layout(c, tn, blk):
    starts, need = [], 0
    for j in range(c // tn):
        lo = (j * tn // blk) * blk
        hi = (((j + 1) * tn - 1) // blk + 1) * blk
        st = (lo // LANES) * LANES
        starts.append(st)
        need = max(need, hi - st)
    kw = -(-need // LANES) * LANES
    starts = [min(st, c - kw) for st in starts]
    return tuple(starts), kw


def _band_weights(gw, starts, kw, tn):
    dense = jax.scipy.linalg.block_diag(*[gw[i] for i in range(gw.shape[0])])
    return jnp.stack([dense[st:st + kw, j * tn:(j + 1) * tn]
                      for j, st in enumerate(starts)]).astype(BF)


def _scan_linear(a, b, h0):
    t = a.shape[0]
    row = lax.broadcasted_iota(jnp.int32, (SUBLANES, a.shape[1]), 0)
    out = []
    carry = h0
    for g in range(t // SUBLANES):
        ag = a[g * SUBLANES:(g + 1) * SUBLANES]
        bg = b[g * SUBLANES:(g + 1) * SUBLANES]
        d = 1
        while d < SUBLANES:
            keep = row >= d
            a_s = jnp.where(keep, pltpu.roll(ag, d, axis=0), 1.0)
            b_s = jnp.where(keep, pltpu.roll(bg, d, axis=0), 0.0)
            bg = ag * b_s + bg
            ag = ag * a_s
            d *= 2
        hg = bg + ag * carry
        carry = hg[SUBLANES - 1:SUBLANES]
        out.append(hg)
    return jnp.concatenate(out, axis=0)


def _rglru_kernel(xb_ref, gb_ref, cw_ref, cb_ref, wa_ref, wx_ref, ba_ref, bx_ref, lam_ref,
                  y_ref, tail_ref, hc_ref, xc_ref, xcb_ref, *, starts, kw, tn):
    t, c = xb_ref.shape

    @pl.when(pl.program_id(1) == 0)
    def _():
        tail_ref[...] = jnp.zeros_like(tail_ref)
        hc_ref[...] = jnp.zeros_like(hc_ref)

    for j in range(c // tn):
        cols = slice(j * tn, (j + 1) * tn)
        x = xb_ref[:, cols].astype(F32)
        xc = _causal_conv(x, tail_ref[:, cols], cw_ref, cols) + cb_ref[:, cols]
        tail_ref[:, cols] = x[t - SUBLANES:t]
        xc_ref[:, cols] = xc
        xcb_ref[:, cols] = xc.astype(BF)

    for j, st in enumerate(starts):
        cols = slice(j * tn, (j + 1) * tn)
        xw = xcb_ref[:, st:st + kw]
        r = jax.nn.sigmoid(jnp.dot(xw, wa_ref[j], preferred_element_type=F32) + ba_ref[:, cols])
        i = jax.nn.sigmoid(jnp.dot(xw, wx_ref[j], preferred_element_type=F32) + bx_ref[:, cols])
        neg_lam = -lam_ref[:, cols]
        softplus = jnp.maximum(neg_lam, 0.0) + jnp.log1p(jnp.exp(-jnp.abs(neg_lam)))
        log_a = (-LRU_C) * r * softplus
        a = jnp.exp(log_a)
        th = jnp.tanh(log_a)
        b_in = jnp.sqrt(-2.0 * th / (1.0 - th)) * (i * xc_ref[:, cols])
        h = _scan_linear(a, b_in, hc_ref[0:1, cols])
        hc_ref[:, cols] = jnp.broadcast_to(h[t - 1:t], (SUBLANES, tn))
        y_ref[:, cols] = (h * jax.nn.gelu(gb_ref[:, cols].astype(F32))).astype(y_ref.dtype)


def _rglru(proj, conv_w, conv_b, gate_a_w, gate_a_b, gate_x_w, gate_x_b, lam, c, t=128, tn=256):
    b, s, _ = proj.shape
    starts, kw = _band_layout(c, tn, c // LRU_BLOCKS)
    wa = _band_weights(gate_a_w, starts, kw, tn)
    wx = _band_weights(gate_x_w, starts, kw, tn)
    nt = c // tn
    row = lambda v: v.reshape(1, c)
    full = lambda shape: pl.BlockSpec(shape, lambda bi, ti: (0,) * len(shape))
    return pl.pallas_call(
        functools.partial(_rglru_kernel, starts=starts, kw=kw, tn=tn),
        grid=(b, s // t),
        in_specs=[pl.BlockSpec((None, t, c), lambda bi, ti: (bi, ti, 0)),
                  pl.BlockSpec((None, t, c), lambda bi, ti: (bi, ti, 1)),
                  full((CONV_W, c)), full((1, c)),
                  full((nt, kw, tn)), full((nt, kw, tn)),
                  full((1, c)), full((1, c)), full((1, c))],
        out_specs=pl.BlockSpec((None, t, c), lambda bi, ti: (bi, ti, 0)),
        out_shape=jax.ShapeDtypeStruct((b, s, c), BF),
        scratch_shapes=[pltpu.VMEM((SUBLANES, c), F32), pltpu.VMEM((SUBLANES, c), F32),
                        pltpu.VMEM((t, c), F32), pltpu.VMEM((t, c), BF)],
        compiler_params=_params(2),
        name="rglru",
    )(proj, proj, conv_w, row(conv_b), wa, wx, row(gate_a_b), row(gate_x_b), row(lam))


def _gdn_prep_kernel(x_ref, ba_ref, cw_ref, alog_ref, dtb_ref, o_ref, bg_ref, tail_ref,
                     *, d_qk, n_heads, tc):
    t, c = x_ref.shape

    @pl.when(pl.program_id(1) == 0)
    def _():
        tail_ref[...] = jnp.zeros_like(tail_ref)

    hd = GDN_HEAD_DIM
    for j in range(c // tc):
        cols = slice(j * tc, (j + 1) * tc)
        x = x_ref[:, cols].astype(F32)
        y = _causal_conv(x, tail_ref[:, cols], cw_ref, cols)
        tail_ref[:, cols] = x[t - SUBLANES:t]
        y = y * jax.nn.sigmoid(y)
        if j * tc < 2 * d_qk:
            scale = hd ** -0.5 if j * tc < d_qk else 1.0
            for hh in range(tc // hd):
                yh = y[:, hh * hd:(hh + 1) * hd]
                ss = jnp.sum(yh * yh, axis=-1, keepdims=True)
                yn = yh * lax.rsqrt(ss + EPS)
                if scale != 1.0:
                    yn = yn * scale
                o_ref[:, j * tc + hh * hd:j * tc + (hh + 1) * hd] = yn.astype(o_ref.dtype)
        else:
            o_ref[:, cols] = y.astype(o_ref.dtype)

    ba = ba_ref[...]
    lane = lax.broadcasted_iota(jnp.int32, ba.shape, 1)
    z = ba + dtb_ref[...]
    softplus = jnp.maximum(z, 0.0) + jnp.log1p(jnp.exp(-jnp.abs(z)))
    g = -jnp.exp(alog_ref[...]) * softplus
    bg_ref[...] = jnp.where(lane < n_heads, jax.nn.sigmoid(ba), g)


def _gdn_prep(proj, ba, conv_w, a_log, dt_bias, d_qk, d_v, t=256, tc=512):
    b, s, _ = proj.shape
    c = 2 * d_qk + d_v
    nh = a_log.shape[0]
    pad = lambda v: jnp.zeros((1, LANES), F32).at[0, nh:2 * nh].set(v)
    full = lambda shape: pl.BlockSpec(shape, lambda bi, ti: (0,) * len(shape))
    return pl.pallas_call(
        functools.partial(_gdn_prep_kernel, d_qk=d_qk, n_heads=nh, tc=tc),
        grid=(b, s // t),
        in_specs=[pl.BlockSpec((None, t, c), lambda bi, ti: (bi, ti, 0)),
                  pl.BlockSpec((None, t, LANES), lambda bi, ti: (bi, ti, 0)),
                  full((CONV_W, c)), full((1, LANES)), full((1, LANES))],
        out_specs=[pl.BlockSpec((None, t, c), lambda bi, ti: (bi, ti, 0)),
                   pl.BlockSpec((None, t, LANES), lambda bi, ti: (bi, ti, 0))],
        out_shape=[jax.ShapeDtypeStruct((b, s, c), BF),
                   jax.ShapeDtypeStruct((b, s, LANES), F32)],
        scratch_shapes=[pltpu.VMEM((SUBLANES, c), F32)],
        compiler_params=_params(2),
        name="gdn_prep",
    )(proj, ba, conv_w, pad(a_log), pad(dt_bias))


def _split2(x):
    hi = x.astype(BF)
    mid = (x - hi.astype(F32)).astype(BF)
    return hi, mid


def _block_diag2(x, half):
    lane = lax.broadcasted_iota(jnp.int32, x.shape, 1)
    zero = jnp.zeros_like(x)
    return jnp.concatenate([jnp.where(lane < half, x, zero), jnp.where(lane >= half, x, zero)], axis=0)


def _alternate(first, second):
    live = [first, second]
    while live:
        for gen in list(live):
            if next(gen, "done") == "done":
                live.remove(gen)


def _gdn_delta_kernel(q_ref, k_ref, v_ref, z_ref, bg_ref, ng_ref, o_ref,
                      state_ref, w_ref, lhs_ref, attn_ref, ks_ref, dec_ref, *, n_heads, pp, cpg):
    ck = GDN_CHUNK
    hd = GDN_HEAD_DIM
    ts = q_ref.shape[0]
    n_groups = ts // (ck * cpg)
    pair0 = pl.program_id(1) * pp

    @pl.when(pl.program_id(2) == 0)
    def _():
        state_ref[...] = jnp.zeros_like(state_ref)

    lane64 = lax.broadcasted_iota(jnp.int32, (ck, 2 * ck), 1)
    row64 = lax.broadcasted_iota(jnp.int32, (ck, 2 * ck), 0)
    col_in_head = jnp.where(lane64 < ck, lane64, lane64 - ck)
    first64 = lane64 < ck
    causal = row64 >= col_in_head
    strict = row64 > col_in_head
    eye = row64 == col_in_head
    eye_f = jnp.where(eye, 1.0, 0.0)
    mask_lo = jnp.where(first64, 1.0, 0.0).astype(BF)
    mask_hi = jnp.where(first64, 0.0, 1.0).astype(BF)
    lane128 = lax.broadcasted_iota(jnp.int32, (ck, 2 * hd), 1)
    first128 = lane128 < hd
    lane_bg = lax.broadcasted_iota(jnp.int32, (ck, LANES), 1)
    dot = functools.partial(jnp.dot, preferred_element_type=F32)
    pairs = range(pp)
    insts = [(cj, pi) for cj in range(cpg) for pi in pairs]
    n_sq = 1
    while (2 << n_sq) < ck:
        n_sq += 1

    def chunk_rows(gi, cj):
        return pl.ds(pl.multiple_of((gi * cpg + cj) * ck, ck), ck)

    def bd_bf(x):
        return jnp.concatenate([x * mask_lo, x * mask_hi], axis=0)

    def precise_rhs(m_hi, m_mid):
        bd_hi = bd_bf(m_hi)
        return jnp.concatenate([bd_hi, bd_hi, bd_bf(m_mid)], axis=0)

    def precise_lhs(x_hi, x_mid):
        return jnp.concatenate([x_hi, x_mid, x_hi], axis=1)

    def prepare(gi, slot0):
        betas, gcs, qs, ks, a_mats = {}, {}, {}, {}, {}
        for cj, pi in insts:
            rows = chunk_rows(gi, cj)
            bg = bg_ref[rows, :]

            def column(l, bg=bg):
                return jnp.sum(jnp.where(lane_bg == l, bg, 0.0), axis=-1, keepdims=True)

            h0 = 2 * (pair0 + pi)
            beta0, beta1 = column(h0), column(h0 + 1)
            gc = jnp.where(first64, column(n_heads + h0), column(n_heads + h0 + 1))
            d = 1
            while d < ck:
                if d < SUBLANES:
                    gc = gc + jnp.where(row64 >= d, pltpu.roll(gc, d, axis=0), 0.0)
                else:
                    gc = jnp.concatenate([gc[:d], gc[d:] + gc[:ck - d]], axis=0)
                d *= 2
            g_row = jnp.sum(jnp.where(eye, gc, 0.0), axis=0, keepdims=True)
            decay = jnp.where(causal, jnp.exp(jnp.where(causal, gc - g_row, 0.0)), 0.0)

            q = q_ref[rows, pi * hd:(pi + 1) * hd]
            k = k_ref[rows, pi * hd:(pi + 1) * hd]
            qk = jnp.concatenate([q, k], axis=0)
            kk = jnp.concatenate([k, k], axis=0)
            prod = lax.dot_general(qk, kk, (((1,), (1,)), ((), ())), preferred_element_type=F32)
            beta64 = jnp.where(first64, beta0, beta1)
            a_mats[cj, pi] = jnp.where(strict, prod[ck:] * beta64 * decay, 0.0)
            attn_ref[slot0 + cj, pi] = jnp.where(causal, prod[:ck] * decay, 0.0).astype(BF)
            betas[cj, pi] = (beta0, beta1)
            gcs[cj, pi] = gc
            qs[cj, pi] = q
            ks[cj, pi] = k
        yield

        p_mats, m_mats, a_split = {}, {}, {}
        for key in insts:
            a_split[key] = _split2(a_mats[key])
            a_hi = a_split[key][0]
            m_mats[key] = dot(a_hi, bd_bf(a_hi))
            p_mats[key] = eye_f - a_mats[key]
        yield
        for step in range(n_sq):
            for key in insts:
                m_bf = m_mats[key].astype(BF)
                p_bf = p_mats[key].astype(BF)
                if step + 1 < n_sq:
                    both = dot(jnp.concatenate([p_bf, m_bf], axis=0), bd_bf(m_bf))
                    p_mats[key] = p_mats[key] + both[:ck]
                    m_mats[key] = both[ck:]
                else:
                    p_mats[key] = p_mats[key] + dot(p_bf, bd_bf(m_bf))
            yield
        t_his, resid = {}, {}
        for key in insts:
            t_hi, t_mid = _split2(p_mats[key])
            a_hi, a_mid = a_split[key]
            a_t = dot(precise_lhs(a_hi, a_mid), precise_rhs(t_hi, t_mid))
            resid[key] = (eye_f - p_mats[key]) - a_t
            t_his[key] = t_hi
        yield
        for key in insts:
            p_mats[key] = p_mats[key] + dot(t_his[key], bd_bf(resid[key].astype(BF)))
        yield

        for cj, pi in insts:
            rows = chunk_rows(gi, cj)
            t_inv = p_mats[cj, pi]
            gc = gcs[cj, pi]
            gc128 = jnp.where(first128, gc[:, 0:1], gc[:, ck:ck + 1])
            gl128 = gc128[ck - 1:ck, :]
            beta128 = jnp.where(first128, betas[cj, pi][0], betas[cj, pi][1])
            e_g = jnp.exp(gc128)
            kf = ks[cj, pi].astype(F32)
            k2 = jnp.concatenate([kf, kf], axis=1)
            qf = qs[cj, pi].astype(F32)
            q2 = jnp.concatenate([qf, qf], axis=1)
            vb = v_ref[rows, pi * 2 * hd:(pi + 1) * 2 * hd].astype(F32) * beta128
            kbg = k2 * beta128 * e_g
            rhs = jnp.concatenate([_block_diag2(vb, hd), _block_diag2(kbg, hd)], axis=1).astype(BF)
            wk = dot(t_inv.astype(BF), rhs)
            slot = slot0 + cj
            w_ref[slot, pi] = wk[:, :2 * hd]
            lhs_ref[slot, pi] = jnp.concatenate([wk[:, 2 * hd:], q2 * e_g], axis=0).astype(BF)
            ks_ref[slot, pi] = _block_diag2(k2 * jnp.exp(gl128 - gc128), hd).astype(BF)
            dec_ref[slot, pi] = jnp.broadcast_to(jnp.exp(gl128), (SUBLANES, 2 * hd))
        yield

    def recur(gi, slot0):
        ng = ng_ref[...]
        for cj in range(cpg):
            slot = slot0 + cj
            rows = chunk_rows(gi, cj)
            states = [state_ref[pi] for pi in pairs]
            ks_qs = [dot(lhs_ref[slot, pi], states[pi].astype(BF)) for pi in pairs]
            yield
            v_bds = [_block_diag2(w_ref[slot, pi] - ks_qs[pi][:ck], hd).astype(BF) for pi in pairs]
            upds = [lax.dot_general(ks_ref[slot, pi], v_bds[pi], (((0,), (0,)), ((), ())),
                                    preferred_element_type=F32) for pi in pairs]
            for pi in pairs:
                state_ref[pi] = states[pi] * dec_ref[slot, pi, 0:1, :] + upds[pi]
            yield
            outs = [ks_qs[pi][ck:] + dot(attn_ref[slot, pi], v_bds[pi]) for pi in pairs]
            for pi in pairs:
                for hh in range(2):
                    sl = slice((2 * pi + hh) * hd, (2 * pi + hh + 1) * hd)
                    oh = outs[pi][:, hh * hd:(hh + 1) * hd]
                    ms = jnp.mean(oh * oh, axis=-1, keepdims=True)
                    zh = z_ref[rows, sl].astype(F32)
                    gate = zh * jax.nn.sigmoid(zh)
                    o_ref[rows, sl] = (oh * lax.rsqrt(ms + EPS) * ng * gate).astype(o_ref.dtype)
            yield

    for _ in prepare(0, 0):
        pass

    def body(g2, carry):
        g_even = 2 * g2
        _alternate(recur(g_even, 0), prepare(g_even + 1, cpg))
        _alternate(recur(g_even + 1, cpg), prepare(g_even + 2, 0))
        return carry

    lax.fori_loop(0, n_groups // 2 - 1, body, 0)
    _alternate(recur(n_groups - 2, 0), prepare(n_groups - 1, cpg))
    for _ in recur(n_groups - 1, cpg):
        pass


def _gdn_delta(qkv, zsrc, z_col0, bg, norm_g, d_qk, d_v, ts=1024, pp=4, cpg=4):
    b, s, _ = qkv.shape
    hd = GDN_HEAD_DIM
    ck = GDN_CHUNK
    n_heads = d_v // hd
    n_pairs = n_heads // 2
    ts = min(ts, s)
    assert d_qk // hd == n_pairs and n_pairs % pp == 0 and s % ts == 0 and ts % (2 * cpg * ck) == 0
    qw = pp * hd
    vw = 2 * pp * hd
    assert d_qk % qw == 0 and (2 * d_qk) % vw == 0 and z_col0 % vw == 0
    k_blk0 = d_qk // qw
    v_blk0 = 2 * d_qk // vw
    z_blk0 = z_col0 // vw
    n_slots = 2 * cpg
    return pl.pallas_call(
        functools.partial(_gdn_delta_kernel, n_heads=n_heads, pp=pp, cpg=cpg),
        grid=(b, n_pairs // pp, s // ts),
        in_specs=[pl.BlockSpec((None, ts, qw), lambda bi, pi, si: (bi, si, pi)),
                  pl.BlockSpec((None, ts, qw), lambda bi, pi, si: (bi, si, k_blk0 + pi)),
                  pl.BlockSpec((None, ts, vw), lambda bi, pi, si: (bi, si, v_blk0 + pi)),
                  pl.BlockSpec((None, ts, vw), lambda bi, pi, si: (bi, si, z_blk0 + pi)),
                  pl.BlockSpec((None, ts, LANES), lambda bi, pi, si: (bi, si, 0)),
                  pl.BlockSpec((1, hd), lambda bi, pi, si: (0, 0))],
        out_specs=pl.BlockSpec((None, ts, vw), lambda bi, pi, si: (bi, si, pi)),
        out_shape=jax.ShapeDtypeStruct((b, s, d_v), BF),
        scratch_shapes=[pltpu.VMEM((pp, 2 * hd, 2 * hd), F32),
                        pltpu.VMEM((n_slots, pp, ck, 2 * hd), F32),
                        pltpu.VMEM((n_slots, pp, 2 * ck, 2 * hd), BF),
                        pltpu.VMEM((n_slots, pp, ck, 2 * ck), BF),
                        pltpu.VMEM((n_slots, pp, 2 * ck, 2 * hd), BF),
                        pltpu.VMEM((n_slots, pp, SUBLANES, 2 * hd), F32)],
        compiler_params=_params(3),
        name="gdn_delta",
    )(qkv, qkv, qkv, zsrc, bg, norm_g.reshape(1, hd))


def kernel(x, mem, norm_mix_g, norm_mem_g, mem_kv_w, norm_ffn_g, ffn_w_in, ffn_w_out, lru_w_in, lru_w_out, lru_conv_w, lru_conv_b, lru_gate_a_w, lru_gate_a_b, lru_gate_x_w, lru_gate_x_b, lru_lambda, gdn_w_in, gdn_w_out, gdn_conv_w, gdn_a_log, gdn_dt_bias, gdn_norm_g, final_norm_g):
    b, s, d = x.shape
    m = b * s
    mem_len = mem.shape[1]
    depth = norm_mix_g.shape[0]
    d_rnn = lru_lambda.shape[1]
    n_vh = gdn_a_log.shape[1]
    d_v = n_vh * GDN_HEAD_DIM
    d_qk = (gdn_conv_w.shape[2] - d_v) // 2
    gdn_main = 2 * d_qk + 2 * d_v
    gdn_w_in_t = jnp.swapaxes(gdn_w_in, 1, 2)

    h = x.reshape(m, d)
    mem2 = mem.reshape(b * mem_len, d)
    hb = None
    for i in range(depth):
        j = i // 2
        hn, gmix = (_rmsnorm(h, norm_mix_g[i], BF), None) if hb is None else (hb, norm_mix_g[i])
        mem_n = _rmsnorm(mem2, norm_mem_g[i], BF)
        kv = _matmul([mem_n], mem_kv_w, i, 0, 2 * d, BF).reshape(b, mem_len, 2 * d)
        if i % 2 == 0:
            proj = _matmul([hn], lru_w_in, j, 0, 2 * d_rnn, BF, tm=1024, gain=gmix).reshape(b, s, 2 * d_rnn)
            xq = _matmul([hn], lru_w_in, j, 2 * d_rnn, d, BF, tm=1024, gain=gmix)
            y = _rglru(proj, lru_conv_w[j], lru_conv_b[j], lru_gate_a_w[j], lru_gate_a_b[j],
                       lru_gate_x_w[j], lru_gate_x_b[j], lru_lambda[j], d_rnn)
            w_out = lru_w_out
        else:
            proj_args = dict(tm=1024, w_t=True, gain=gmix)
            proj = _matmul([hn], gdn_w_in_t, j, 0, gdn_main, BF, tn=1024,
                           **proj_args).reshape(b, s, gdn_main)
            w_tail = _row_window(gdn_w_in_t, j, gdn_main, 2 * n_vh)
            w_ba = jnp.pad(w_tail[:2 * n_vh], ((0, LANES - 2 * n_vh), (0, 0)))[None]
            ba = _matmul([hn], w_ba, 0, 0, LANES, F32, **proj_args).reshape(b, s, LANES)
            xq = _matmul([hn], w_tail[None, 2 * n_vh:], 0, 0, d, BF, tn=1024, **proj_args)
            qkv, bg = _gdn_prep(proj, ba, gdn_conv_w[j], gdn_a_log[j], gdn_dt_bias[j], d_qk, d_v)
            y = _gdn_delta(qkv, proj, 2 * d_qk + d_v, bg, gdn_norm_g[j], d_qk, d_v)
            w_out = gdn_w_out
        xo = _xattn(xq.reshape(b, s, d), kv)
        y2 = y.reshape(m, y.shape[-1])
        h, hb = _matmul([y2, xo.reshape(m, d)], w_out, j, 0, d, F32, residual=h, emit_bf16=True)
        act = _ffn_in(hb, ffn_w_in, i, norm_ffn_g[i])
        last = i + 1 == depth
        out = _matmul([act], ffn_w_out, i, 0, d, F32, residual=h, emit_bf16=not last)
        h, hb = (out, None) if last else out
    return _rmsnorm(h, final_norm_g, F32).reshape(b, s, d)
```

```python
import functools

import jax
import jax.numpy as jnp
from jax import lax
from jax.experimental import pallas as pl
from jax.experimental.pallas import tpu as pltpu

F32 = jnp.float32
BF = jnp.bfloat16

EPS = 1e-6
CONV_W = 4
LRU_BLOCKS = 16
LRU_C = 8.0
GDN_HEAD_DIM = 128
GDN_CHUNK = 64
X_HEADS = 4

LANES = 128
SUBLANES = 8
VMEM_LIMIT = 56 * 1024 * 1024


def _params(n_axes):
    return pltpu.CompilerParams(
        dimension_semantics=("arbitrary",) * n_axes, vmem_limit_bytes=VMEM_LIMIT)


def _rmsnorm_kernel(x_ref, g_ref, o_ref):
    x = x_ref[...]
    ms = jnp.mean(x * x, axis=-1, keepdims=True)
    o_ref[...] = (x * lax.rsqrt(ms + EPS) * g_ref[...]).astype(o_ref.dtype)


def _rmsnorm(x, g, out_dtype, tm=512):
    m, d = x.shape
    tm = min(tm, m)
    return pl.pallas_call(
        _rmsnorm_kernel,
        grid=(m // tm,),
        in_specs=[pl.BlockSpec((tm, d), lambda i: (i, 0)),
                  pl.BlockSpec((1, d), lambda i: (0, 0))],
        out_specs=pl.BlockSpec((tm, d), lambda i: (i, 0)),
        out_shape=jax.ShapeDtypeStruct((m, d), out_dtype),
        compiler_params=_params(1),
        name="rmsnorm",
    )(x, g.reshape(1, d))


_CAST_ROWS = 512


def _cast_weight(w_ref, wb_ref, g_ref=None):
    k = w_ref.shape[0]
    rows = _CAST_ROWS if k % _CAST_ROWS == 0 else k

    def body(r, carry):
        sl = pl.ds(pl.multiple_of(r * rows, rows), rows)
        blk = w_ref[sl, :]
        if g_ref is not None:
            blk = blk * g_ref[sl, :]
        wb_ref[sl, :] = blk.astype(BF)
        return carry

    lax.fori_loop(0, k // rows, body, 0)


def _cast_weight_t(wt_ref, wb_ref, g_ref=None):
    k = wt_ref.shape[1]
    step = _CAST_ROWS if k % _CAST_ROWS == 0 else k
    for r in range(k // step):
        sl = slice(r * step, (r + 1) * step)
        blk = wt_ref[:, sl].T
        if g_ref is not None:
            blk = blk * g_ref[sl, :]
        wb_ref[sl, :] = blk.astype(BF)


def _row_rms_scale(x_ref, r_ref, rows):
    @pl.when(pl.program_id(0) == 0)
    def _():
        xf = x_ref[...].astype(F32)
        r = lax.rsqrt(jnp.mean(xf * xf, axis=-1, keepdims=True) + EPS)
        r_ref[rows, :] = jnp.broadcast_to(r, (r.shape[0], LANES))


def _scale_rows(acc, r):
    reps = acc.shape[1] // LANES
    return acc * (jnp.concatenate([r] * reps, axis=1) if reps > 1 else r)


def _mm_kernel(*refs, k_sizes, has_res, w_t, norm, emit_bf16):
    n_x = len(k_sizes)
    x_refs = refs[:n_x]
    pos = n_x
    w_ref = refs[pos]
    pos += 1
    res_ref = refs[pos] if has_res else None
    pos += has_res
    g_ref = refs[pos] if norm else None
    pos += norm
    o_ref = refs[pos]
    pos += 1
    ob_ref = refs[pos] if emit_bf16 else None
    pos += emit_bf16
    wb_ref = refs[pos]
    r_ref = refs[pos + 1] if norm else None
    tm = o_ref.shape[0]
    rows = pl.ds(pl.multiple_of(pl.program_id(1) * tm, tm), tm)

    @pl.when(pl.program_id(1) == 0)
    def _():
        if w_t:
            _cast_weight_t(w_ref, wb_ref, g_ref)
        else:
            _cast_weight(w_ref, wb_ref, g_ref)

    if norm:
        _row_rms_scale(x_refs[0], r_ref, rows)

    acc = None
    off = 0
    for x_ref, k in zip(x_refs, k_sizes):
        part = jnp.dot(x_ref[...], wb_ref[off:off + k, :], preferred_element_type=F32)
        acc = part if acc is None else acc + part
        off += k
    if norm:
        acc = _scale_rows(acc, r_ref[rows, :])
    if has_res:
        acc = acc + res_ref[...]
    o_ref[...] = acc.astype(o_ref.dtype)
    if emit_bf16:
        ob_ref[...] = acc.astype(BF)


def _matmul(xs, w, layer, col0, n_cols, out_dtype, residual=None, tm=512, tn=512, w_t=False,
            gain=None, emit_bf16=False):
    m = xs[0].shape[0]
    k_sizes = tuple(x.shape[1] for x in xs)
    k_tot = sum(k_sizes)
    norm = gain is not None
    assert w.shape[2 if w_t else 1] == k_tot and not (norm and (residual is not None or len(xs) > 1))
    tm = min(tm, m)
    tn = min(tn, n_cols)
    assert m % tm == 0 and n_cols % tn == 0 and col0 % tn == 0
    cb0 = col0 // tn
    in_specs = [pl.BlockSpec((tm, k), lambda j, i: (i, 0)) for k in k_sizes]
    if w_t:
        in_specs.append(pl.BlockSpec((None, tn, k_tot), lambda j, i: (layer, j + cb0, 0)))
    else:
        in_specs.append(pl.BlockSpec((None, k_tot, tn), lambda j, i: (layer, 0, j + cb0)))
    args = list(xs) + [w]
    if residual is not None:
        in_specs.append(pl.BlockSpec((tm, tn), lambda j, i: (i, j)))
        args.append(residual)
    scratch = [pltpu.VMEM((k_tot, tn), BF)]
    if norm:
        in_specs.append(pl.BlockSpec((k_tot, 1), lambda j, i: (0, 0)))
        args.append(gain.reshape(k_tot, 1))
        scratch.append(pltpu.VMEM((m, LANES), F32))
    out_spec = pl.BlockSpec((tm, tn), lambda j, i: (i, j))
    out_shape = jax.ShapeDtypeStruct((m, n_cols), out_dtype)
    return pl.pallas_call(
        functools.partial(_mm_kernel, k_sizes=k_sizes, has_res=residual is not None, w_t=w_t,
                          norm=norm, emit_bf16=emit_bf16),
        grid=(n_cols // tn, m // tm),
        in_specs=in_specs,
        out_specs=[out_spec, out_spec] if emit_bf16 else out_spec,
        out_shape=[out_shape, jax.ShapeDtypeStruct((m, n_cols), BF)] if emit_bf16 else out_shape,
        scratch_shapes=scratch,
        compiler_params=_params(2),
        name="matmul",
    )(*args)


def _copy_kernel(w_ref, o_ref):
    o_ref[...] = w_ref[...]


def _row_window(w, layer, row0, rows):
    _, n, k = w.shape
    assert row0 % rows == 0 and (n - row0) % rows == 0 and rows % SUBLANES == 0
    rb0 = row0 // rows
    return pl.pallas_call(
        _copy_kernel,
        grid=((n - row0) // rows,),
        in_specs=[pl.BlockSpec((None, rows, k), lambda j: (layer, j + rb0, 0))],
        out_specs=pl.BlockSpec((rows, k), lambda j: (j, 0)),
        out_shape=jax.ShapeDtypeStruct((n - row0, k), w.dtype),
        compiler_params=_params(1),
        name="row_window",
    )(w)


def _ffn_in_kernel(x_ref, wg_ref, wu_ref, g_ref, o_ref, wgb_ref, wub_ref, r_ref):
    tm = o_ref.shape[0]
    rows = pl.ds(pl.multiple_of(pl.program_id(1) * tm, tm), tm)

    @pl.when(pl.program_id(1) == 0)
    def _():
        _cast_weight(wg_ref, wgb_ref, g_ref)
        _cast_weight(wu_ref, wub_ref, g_ref)

    _row_rms_scale(x_ref, r_ref, rows)
    x = x_ref[...]
    r = r_ref[rows, :]
    gate = _scale_rows(jnp.dot(x, wgb_ref[...], preferred_element_type=F32), r)
    up = _scale_rows(jnp.dot(x, wub_ref[...], preferred_element_type=F32), r)
    o_ref[...] = (gate * jax.nn.sigmoid(gate) * up).astype(o_ref.dtype)


def _ffn_in(x, w, layer, gain, tm=1024, tn=512):
    m, k = x.shape
    f = w.shape[2] // 2
    assert m % tm == 0 and f % tn == 0
    nb = f // tn
    return pl.pallas_call(
        _ffn_in_kernel,
        grid=(nb, m // tm),
        in_specs=[pl.BlockSpec((tm, k), lambda j, i: (i, 0)),
                  pl.BlockSpec((None, k, tn), lambda j, i: (layer, 0, j)),
                  pl.BlockSpec((None, k, tn), lambda j, i: (layer, 0, j + nb)),
                  pl.BlockSpec((k, 1), lambda j, i: (0, 0))],
        out_specs=pl.BlockSpec((tm, tn), lambda j, i: (i, j)),
        out_shape=jax.ShapeDtypeStruct((m, f), BF),
        scratch_shapes=[pltpu.VMEM((k, tn), BF), pltpu.VMEM((k, tn), BF), pltpu.VMEM((m, LANES), F32)],
        compiler_params=_params(2),
        name="ffn_in",
    )(x, w, w, gain.reshape(k, 1))


def _xattn_kernel(q_ref, k_ref, v_ref, o_ref, *, heads, hd):
    scale = hd ** -0.5
    for h in range(heads):
        sl = slice(h * hd, (h + 1) * hd)
        s = lax.dot_general(q_ref[:, sl], k_ref[:, sl], (((1,), (1,)), ((), ())),
                            preferred_element_type=F32) * scale
        mx = jnp.max(s, axis=-1, keepdims=True)
        p = jnp.exp(s - mx)
        denom = jnp.sum(p, axis=-1, keepdims=True)
        o = jnp.dot(p.astype(BF), v_ref[:, sl], preferred_element_type=F32)
        o_ref[:, sl] = (o / denom).astype(o_ref.dtype)


def _xattn(q, kv, ts=512):
    b, s, d = q.shape
    mem = kv.shape[1]
    return pl.pallas_call(
        functools.partial(_xattn_kernel, heads=X_HEADS, hd=d // X_HEADS),
        grid=(b, s // ts),
        in_specs=[pl.BlockSpec((None, ts, d), lambda bi, si: (bi, si, 0)),
                  pl.BlockSpec((None, mem, d), lambda bi, si: (bi, 0, 0)),
                  pl.BlockSpec((None, mem, d), lambda bi, si: (bi, 0, 1))],
        out_specs=pl.BlockSpec((None, ts, d), lambda bi, si: (bi, si, 0)),
        out_shape=jax.ShapeDtypeStruct((b, s, d), BF),
        compiler_params=_params(2),
        name="xattn",
    )(q, kv, kv)


def _causal_conv(x, tail, cw_ref, cols):
    t = x.shape[0]
    row8 = lax.broadcasted_iota(jnp.int32, (SUBLANES, x.shape[1]), 0)
    acc = x * cw_ref[CONV_W - 1:CONV_W, cols]
    for d in range(1, CONV_W):
        xr = pltpu.roll(x, d, axis=0)
        head = jnp.where(row8 < d, pltpu.roll(tail, d, axis=0), xr[0:SUBLANES])
        xs = jnp.concatenate([head, xr[SUBLANES:t]], axis=0)
        acc = acc + xs * cw_ref[CONV_W - 1 - d:CONV_W - d, cols]
    return acc


def _band_layout(c, tn, blk):
    starts, need = [], 0
    for j in range(c // tn):
        lo = (j * tn // blk) * blk
        hi = (((j + 1) * tn - 1) // blk + 1) * blk
        st = (lo // LANES) * LANES
        starts.append(st)
        need = max(need, hi - st)
    kw = -(-need // LANES) * LANES
    starts = [min(st, c - kw) for st in starts]
    return tuple(starts), kw


def _band_weights(gw, starts, kw, tn):
    dense = jax.scipy.linalg.block_diag(*[gw[i] for i in range(gw.shape[0])])
    return jnp.stack([dense[st:st + kw, j * tn:(j + 1) * tn]
                      for j, st in enumerate(starts)]).astype(BF)


def _scan_linear(a, b, h0):
    t = a.shape[0]
    row = lax.broadcasted_iota(jnp.int32, (SUBLANES, a.shape[1]), 0)
    out = []
    carry = h0
    for g in range(t // SUBLANES):
        ag = a[g * SUBLANES:(g + 1) * SUBLANES]
        bg = b[g * SUBLANES:(g + 1) * SUBLANES]
        d = 1
        while d < SUBLANES:
            keep = row >= d
            a_s = jnp.where(keep, pltpu.roll(ag, d, axis=0), 1.0)
            b_s = jnp.where(keep, pltpu.roll(bg, d, axis=0), 0.0)
            bg = ag * b_s + bg
            ag = ag * a_s
            d *= 2
        hg = bg + ag * carry
        carry = hg[SUBLANES - 1:SUBLANES]
        out.append(hg)
    return jnp.concatenate(out, axis=0)


def _rglru_kernel(xb_ref, gb_ref, cw_ref, cb_ref, wa_ref, wx_ref, ba_ref, bx_ref, lam_ref,
                  y_ref, tail_ref, hc_ref, xc_ref, xcb_ref, *, starts, kw, tn):
    t, c = xb_ref.shape

    @pl.when(pl.program_id(1) == 0)
    def _():
        tail_ref[...] = jnp.zeros_like(tail_ref)
        hc_ref[...] = jnp.zeros_like(hc_ref)

    for j in range(c // tn):
        cols = slice(j * tn, (j + 1) * tn)
        x = xb_ref[:, cols].astype(F32)
        xc = _causal_conv(x, tail_ref[:, cols], cw_ref, cols) + cb_ref[:, cols]
        tail_ref[:, cols] = x[t - SUBLANES:t]
        xc_ref[:, cols] = xc
        xcb_ref[:, cols] = xc.astype(BF)

    for j, st in enumerate(starts):
        cols = slice(j * tn, (j + 1) * tn)
        xw = xcb_ref[:, st:st + kw]
        r = jax.nn.sigmoid(jnp.dot(xw, wa_ref[j], preferred_element_type=F32) + ba_ref[:, cols])
        i = jax.nn.sigmoid(jnp.dot(xw, wx_ref[j], preferred_element_type=F32) + bx_ref[:, cols])
        neg_lam = -lam_ref[:, cols]
        softplus = jnp.maximum(neg_lam, 0.0) + jnp.log1p(jnp.exp(-jnp.abs(neg_lam)))
        log_a = (-LRU_C) * r * softplus
        a = jnp.exp(log_a)
        th = jnp.tanh(log_a)
        b_in = jnp.sqrt(-2.0 * th / (1.0 - th)) * (i * xc_ref[:, cols])
        h = _scan_linear(a, b_in, hc_ref[0:1, cols])
        hc_ref[:, cols] = jnp.broadcast_to(h[t - 1:t], (SUBLANES, tn))
        y_ref[:, cols] = (h * jax.nn.gelu(gb_ref[:, cols].astype(F32))).astype(y_ref.dtype)


def _rglru(proj, conv_w, conv_b, gate_a_w, gate_a_b, gate_x_w, gate_x_b, lam, c, t=128, tn=256):
    b, s, _ = proj.shape
    starts, kw = _band_layout(c, tn, c // LRU_BLOCKS)
    wa = _band_weights(gate_a_w, starts, kw, tn)
    wx = _band_weights(gate_x_w, starts, kw, tn)
    nt = c // tn
    row = lambda v: v.reshape(1, c)
    full = lambda shape: pl.BlockSpec(shape, lambda bi, ti: (0,) * len(shape))
    return pl.pallas_call(
        functools.partial(_rglru_kernel, starts=starts, kw=kw, tn=tn),
        grid=(b, s // t),
        in_specs=[pl.BlockSpec((None, t, c), lambda bi, ti: (bi, ti, 0)),
                  pl.BlockSpec((None, t, c), lambda bi, ti: (bi, ti, 1)),
                  full((CONV_W, c)), full((1, c)),
                  full((nt, kw, tn)), full((nt, kw, tn)),
                  full((1, c)), full((1, c)), full((1, c))],
        out_specs=pl.BlockSpec((None, t, c), lambda bi, ti: (bi, ti, 0)),
        out_shape=jax.ShapeDtypeStruct((b, s, c), BF),
        scratch_shapes=[pltpu.VMEM((SUBLANES, c), F32), pltpu.VMEM((SUBLANES, c), F32),
                        pltpu.VMEM((t, c), F32), pltpu.VMEM((t, c), BF)],
        compiler_params=_params(2),
        name="rglru",
    )(proj, proj, conv_w, row(conv_b), wa, wx, row(gate_a_b), row(gate_x_b), row(lam))


def _gdn_prep_kernel(x_ref, ba_ref, cw_ref, alog_ref, dtb_ref, o_ref, bg_ref, tail_ref,
                     *, d_qk, n_heads, tc):
    t, c = x_ref.shape

    @pl.when(pl.program_id(1) == 0)
    def _():
        tail_ref[...] = jnp.zeros_like(tail_ref)

    hd = GDN_HEAD_DIM
    for j in range(c // tc):
        cols = slice(j * tc, (j + 1) * tc)
        x = x_ref[:, cols].astype(F32)
        y = _causal_conv(x, tail_ref[:, cols], cw_ref, cols)
        tail_ref[:, cols] = x[t - SUBLANES:t]
        y = y * jax.nn.sigmoid(y)
        if j * tc < 2 * d_qk:
            scale = hd ** -0.5 if j * tc < d_qk else 1.0
            for hh in range(tc // hd):
                yh = y[:, hh * hd:(hh + 1) * hd]
                ss = jnp.sum(yh * yh, axis=-1, keepdims=True)
                yn = yh * lax.rsqrt(ss + EPS)
                if scale != 1.0:
                    yn = yn * scale
                o_ref[:, j * tc + hh * hd:j * tc + (hh + 1) * hd] = yn.astype(o_ref.dtype)
        else:
            o_ref[:, cols] = y.astype(o_ref.dtype)

    ba = ba_ref[...]
    lane = lax.broadcasted_iota(jnp.int32, ba.shape, 1)
    z = ba + dtb_ref[...]
    softplus = jnp.maximum(z, 0.0) + jnp.log1p(jnp.exp(-jnp.abs(z)))
    g = -jnp.exp(alog_ref[...]) * softplus
    bg_ref[...] = jnp.where(lane < n_heads, jax.nn.sigmoid(ba), g)


def _gdn_prep(proj, ba, conv_w, a_log, dt_bias, d_qk, d_v, t=256, tc=512):
    b, s, _ = proj.shape
    c = 2 * d_qk + d_v
    nh = a_log.shape[0]
    pad = lambda v: jnp.zeros((1, LANES), F32).at[0, nh:2 * nh].set(v)
    full = lambda shape: pl.BlockSpec(shape, lambda bi, ti: (0,) * len(shape))
    return pl.pallas_call(
        functools.partial(_gdn_prep_kernel, d_qk=d_qk, n_heads=nh, tc=tc),
        grid=(b, s // t),
        in_specs=[pl.BlockSpec((None, t, c), lambda bi, ti: (bi, ti, 0)),
                  pl.BlockSpec((None, t, LANES), lambda bi, ti: (bi, ti, 0)),
                  full((CONV_W, c)), full((1, LANES)), full((1, LANES))],
        out_specs=[pl.BlockSpec((None, t, c), lambda bi, ti: (bi, ti, 0)),
                   pl.BlockSpec((None, t, LANES), lambda bi, ti: (bi, ti, 0))],
        out_shape=[jax.ShapeDtypeStruct((b, s, c), BF),
                   jax.ShapeDtypeStruct((b, s, LANES), F32)],
        scratch_shapes=[pltpu.VMEM((SUBLANES, c), F32)],
        compiler_params=_params(2),
        name="gdn_prep",
    )(proj, ba, conv_w, pad(a_log), pad(dt_bias))


def _split2(x):
    hi = x.astype(BF)
    mid = (x - hi.astype(F32)).astype(BF)
    return hi, mid


def _block_diag2(x, half):
    lane = lax.broadcasted_iota(jnp.int32, x.shape, 1)
    zero = jnp.zeros_like(x)
    return jnp.concatenate([jnp.where(lane < half, x, zero), jnp.where(lane >= half, x, zero)], axis=0)


def _alternate(first, second):
    live = [first, second]
    while live:
        for gen in list(live):
            if next(gen, "done") == "done":
                live.remove(gen)


def _gdn_delta_kernel(q_ref, k_ref, v_ref, z_ref, bg_ref, ng_ref, o_ref,
                      state_ref, w_ref, lhs_ref, attn_ref, ks_ref, dec_ref, *, n_heads, pp, cpg):
    ck = GDN_CHUNK
    hd = GDN_HEAD_DIM
    ts = q_ref.shape[0]
    n_groups = ts // (ck * cpg)
    pair0 = pl.program_id(1) * pp

    @pl.when(pl.program_id(2) == 0)
    def _():
        state_ref[...] = jnp.zeros_like(state_ref)

    lane64 = lax.broadcasted_iota(jnp.int32, (ck, 2 * ck), 1)
    row64 = lax.broadcasted_iota(jnp.int32, (ck, 2 * ck), 0)
    col_in_head = jnp.where(lane64 < ck, lane64, lane64 - ck)
    first64 = lane64 < ck
    causal = row64 >= col_in_head
    strict = row64 > col_in_head
    eye = row64 == col_in_head
    eye_f = jnp.where(eye, 1.0, 0.0)
    mask_lo = jnp.where(first64, 1.0, 0.0).astype(BF)
    mask_hi = jnp.where(first64, 0.0, 1.0).astype(BF)
    lane128 = lax.broadcasted_iota(jnp.int32, (ck, 2 * hd), 1)
    first128 = lane128 < hd
    lane_bg = lax.broadcasted_iota(jnp.int32, (ck, LANES), 1)
    dot = functools.partial(jnp.dot, preferred_element_type=F32)
    pairs = range(pp)
    insts = [(cj, pi) for cj in range(cpg) for pi in pairs]
    n_sq = 1
    while (2 << n_sq) < ck:
        n_sq += 1

    def chunk_rows(gi, cj):
        return pl.ds(pl.multiple_of((gi * cpg + cj) * ck, ck), ck)

    def bd_bf(x):
        return jnp.concatenate([x * mask_lo, x * mask_hi], axis=0)

    def precise_rhs(m_hi, m_mid):
        bd_hi = bd_bf(m_hi)
        return jnp.concatenate([bd_hi, bd_hi, bd_bf(m_mid)], axis=0)

    def precise_lhs(x_hi, x_mid):
        return jnp.concatenate([x_hi, x_mid, x_hi], axis=1)

    def prepare(gi, slot0):
        betas, gcs, qs, ks, a_mats = {}, {}, {}, {}, {}
        for cj, pi in insts:
            rows = chunk_rows(gi, cj)
            bg = bg_ref[rows, :]

            def column(l, bg=bg):
                return jnp.sum(jnp.where(lane_bg == l, bg, 0.0), axis=-1, keepdims=True)

            h0 = 2 * (pair0 + pi)
            beta0, beta1 = column(h0), column(h0 + 1)
            gc = jnp.where(first64, column(n_heads + h0), column(n_heads + h0 + 1))
            d = 1
            while d < ck:
                if d < SUBLANES:
                    gc = gc + jnp.where(row64 >= d, pltpu.roll(gc, d, axis=0), 0.0)
                else:
                    gc = jnp.concatenate([gc[:d], gc[d:] + gc[:ck - d]], axis=0)
                d *= 2
            g_row = jnp.sum(jnp.where(eye, gc, 0.0), axis=0, keepdims=True)
            decay = jnp.where(causal, jnp.exp(jnp.where(causal, gc - g_row, 0.0)), 0.0)

            q = q_ref[rows, pi * hd:(pi + 1) * hd]
            k = k_ref[rows, pi * hd:(pi + 1) * hd]
            qk = jnp.concatenate([q, k], axis=0)
            kk = jnp.concatenate([k, k], axis=0)
            prod = lax.dot_general(qk, kk, (((1,), (1,)), ((), ())), preferred_element_type=F32)
            beta64 = jnp.where(first64, beta0, beta1)
            a_mats[cj, pi] = jnp.where(strict, prod[ck:] * beta64 * decay, 0.0)
            attn_ref[slot0 + cj, pi] = jnp.where(causal, prod[:ck] * decay, 0.0).astype(BF)
            betas[cj, pi] = (beta0, beta1)
            gcs[cj, pi] = gc
            qs[cj, pi] = q
            ks[cj, pi] = k
        yield

        p_mats, m_mats, a_split = {}, {}, {}
        for key in insts:
            a_split[key] = _split2(a_mats[key])
            a_hi = a_split[key][0]
            m_mats[key] = dot(a_hi, bd_bf(a_hi))
            p_mats[key] = eye_f - a_mats[key]
        yield
        for step in range(n_sq):
            for key in insts:
                m_bf = m_mats[key].astype(BF)
                p_bf = p_mats[key].astype(BF)
                if step + 1 < n_sq:
                    both = dot(jnp.concatenate([p_bf, m_bf], axis=0), bd_bf(m_bf))
                    p_mats[key] = p_mats[key] + both[:ck]
                    m_mats[key] = both[ck:]
                else:
                    p_mats[key] = p_mats[key] + dot(p_bf, bd_bf(m_bf))
            yield
        t_his, resid = {}, {}
        for key in insts:
            t_hi, t_mid = _split2(p_mats[key])
            a_hi, a_mid = a_split[key]
            a_t = dot(precise_lhs(a_hi, a_mid), precise_rhs(t_hi, t_mid))
            resid[key] = (eye_f - p_mats[key]) - a_t
            t_his[key] = t_hi
        yield
        for key in insts:
            p_mats[key] = p_mats[key] + dot(t_his[key], bd_bf(resid[key].astype(BF)))
        yield

        for cj, pi in insts:
            rows = chunk_rows(gi, cj)
            t_inv = p_mats[cj, pi]
            gc = gcs[cj, pi]
            gc128 = jnp.where(first128, gc[:, 0:1], gc[:, ck:ck + 1])
            gl128 = gc128[ck - 1:ck, :]
            beta128 = jnp.where(first128, betas[cj, pi][0], betas[cj, pi][1])
            e_g = jnp.exp(gc128)
            kf = ks[cj, pi].astype(F32)
            k2 = jnp.concatenate([kf, kf], axis=1)
            qf = qs[cj, pi].astype(F32)
            q2 = jnp.concatenate([qf, qf], axis=1)
            vb = v_ref[rows, pi * 2 * hd:(pi + 1) * 2 * hd].astype(F32) * beta128
            kbg = k2 * beta128 * e_g
            rhs = jnp.concatenate([_block_diag2(vb, hd), _block_diag2(kbg, hd)], axis=1).astype(BF)
            wk = dot(t_inv.astype(BF), rhs)
            slot = slot0 + cj
            w_ref[slot, pi] = wk[:, :2 * hd]
            lhs_ref[slot, pi] = jnp.concatenate([wk[:, 2 * hd:], q2 * e_g], axis=0).astype(BF)
            ks_ref[slot, pi] = _block_diag2(k2 * jnp.exp(gl128 - gc128), hd).astype(BF)
            dec_ref[slot, pi] = jnp.broadcast_to(jnp.exp(gl128), (SUBLANES, 2 * hd))
        yield

    def recur(gi, slot0):
        ng = ng_ref[...]
        for cj in range(cpg):
            slot = slot0 + cj
            rows = chunk_rows(gi, cj)
            states = [state_ref[pi] for pi in pairs]
            ks_qs = [dot(lhs_ref[slot, pi], states[pi].astype(BF)) for pi in pairs]
            yield
            v_bds = [_block_diag2(w_ref[slot, pi] - ks_qs[pi][:ck], hd).astype(BF) for pi in pairs]
            upds = [lax.dot_general(ks_ref[slot, pi], v_bds[pi], (((0,), (0,)), ((), ())),
                                    preferred_element_type=F32) for pi in pairs]
            for pi in pairs:
                state_ref[pi] = states[pi] * dec_ref[slot, pi, 0:1, :] + upds[pi]
            yield
            outs = [ks_qs[pi][ck:] + dot(attn_ref[slot, pi], v_bds[pi]) for pi in pairs]
            for pi in pairs:
                for hh in range(2):
                    sl = slice((2 * pi + hh) * hd, (2 * pi + hh + 1) * hd)
                    oh = outs[pi][:, hh * hd:(hh + 1) * hd]
                    ms = jnp.mean(oh * oh, axis=-1, keepdims=True)
                    zh = z_ref[rows, sl].astype(F32)
                    gate = zh * jax.nn.sigmoid(zh)
                    o_ref[rows, sl] = (oh * lax.rsqrt(ms + EPS) * ng * gate).astype(o_ref.dtype)
            yield

    for _ in prepare(0, 0):
        pass

    def body(g2, carry):
        g_even = 2 * g2
        _alternate(recur(g_even, 0), prepare(g_even + 1, cpg))
        _alternate(recur(g_even + 1, cpg), prepare(g_even + 2, 0))
        return carry

    lax.fori_loop(0, n_groups // 2 - 1, body, 0)
    _alternate(recur(n_groups - 2, 0), prepare(n_groups - 1, cpg))
    for _ in recur(n_groups - 1, cpg):
        pass


def _gdn_delta(qkv, zsrc, z_col0, bg, norm_g, d_qk, d_v, ts=1024, pp=4, cpg=4):
    b, s, _ = qkv.shape
    hd = GDN_HEAD_DIM
    ck = GDN_CHUNK
    n_heads = d_v // hd
    n_pairs = n_heads // 2
    ts = min(ts, s)
    assert d_qk // hd == n_pairs and n_pairs % pp == 0 and s % ts == 0 and ts % (2 * cpg * ck) == 0
    qw = pp * hd
    vw = 2 * pp * hd
    assert d_qk % qw == 0 and (2 * d_qk) % vw == 0 and z_col0 % vw == 0
    k_blk0 = d_qk // qw
    v_blk0 = 2 * d_qk // vw
    z_blk0 = z_col0 // vw
    n_slots = 2 * cpg
    return pl.pallas_call(
        functools.partial(_gdn_delta_kernel, n_heads=n_heads, pp=pp, cpg=cpg),
        grid=(b, n_pairs // pp, s // ts),
        in_specs=[pl.BlockSpec((None, ts, qw), lambda bi, pi, si: (bi, si, pi)),
                  pl.BlockSpec((None, ts, qw), lambda bi, pi, si: (bi, si, k_blk0 + pi)),
                  pl.BlockSpec((None, ts, vw), lambda bi, pi, si: (bi, si, v_blk0 + pi)),
                  pl.BlockSpec((None, ts, vw), lambda bi, pi, si: (bi, si, z_blk0 + pi)),
                  pl.BlockSpec((None, ts, LANES), lambda bi, pi, si: (bi, si, 0)),
                  pl.BlockSpec((1, hd), lambda bi, pi, si: (0, 0))],
        out_specs=pl.BlockSpec((None, ts, vw), lambda bi, pi, si: (bi, si, pi)),
        out_shape=jax.ShapeDtypeStruct((b, s, d_v), BF),
        scratch_shapes=[pltpu.VMEM((pp, 2 * hd, 2 * hd), F32),
                        pltpu.VMEM((n_slots, pp, ck, 2 * hd), F32),
                        pltpu.VMEM((n_slots, pp, 2 * ck, 2 * hd), BF),
                        pltpu.VMEM((n_slots, pp, ck, 2 * ck), BF),
                        pltpu.VMEM((n_slots, pp, 2 * ck, 2 * hd), BF),
                        pltpu.VMEM((n_slots, pp, SUBLANES, 2 * hd), F32)],
        compiler_params=_params(3),
        name="gdn_delta",
    )(qkv, qkv, qkv, zsrc, bg, norm_g.reshape(1, hd))


def kernel(x, mem, norm_mix_g, norm_mem_g, mem_kv_w, norm_ffn_g, ffn_w_in, ffn_w_out, lru_w_in, lru_w_out, lru_conv_w, lru_conv_b, lru_gate_a_w, lru_gate_a_b, lru_gate_x_w, lru_gate_x_b, lru_lambda, gdn_w_in, gdn_w_out, gdn_conv_w, gdn_a_log, gdn_dt_bias, gdn_norm_g, final_norm_g):
    b, s, d = x.shape
    m = b * s
    mem_len = mem.shape[1]
    depth = norm_mix_g.shape[0]
    d_rnn = lru_lambda.shape[1]
    n_vh = gdn_a_log.shape[1]
    d_v = n_vh * GDN_HEAD_DIM
    d_qk = (gdn_conv_w.shape[2] - d_v) // 2
    gdn_main = 2 * d_qk + 2 * d_v
    gdn_w_in_t = jnp.swapaxes(gdn_w_in, 1, 2)

    h = x.reshape(m, d)
    mem2 = mem.reshape(b * mem_len, d)
    hb = None
    for i in range(depth):
        j = i // 2
        hn, gmix = (_rmsnorm(h, norm_mix_g[i], BF), None) if hb is None else (hb, norm_mix_g[i])
        mem_n = _rmsnorm(mem2, norm_mem_g[i], BF)
        kv = _matmul([mem_n], mem_kv_w, i, 0, 2 * d, BF).reshape(b, mem_len, 2 * d)
        if i % 2 == 0:
            proj = _matmul([hn], lru_w_in, j, 0, 2 * d_rnn, BF, tm=1024, tn=d_rnn // 2,
                           gain=gmix).reshape(b, s, 2 * d_rnn)
            xq = _matmul([hn], lru_w_in, j, 2 * d_rnn, d, BF, tm=1024, gain=gmix)
            y = _rglru(proj, lru_conv_w[j], lru_conv_b[j], lru_gate_a_w[j], lru_gate_a_b[j],
                       lru_gate_x_w[j], lru_gate_x_b[j], lru_lambda[j], d_rnn)
            w_out = lru_w_out
        else:
            proj_args = dict(tm=1024, w_t=True, gain=gmix)
            proj = _matmul([hn], gdn_w_in_t, j, 0, gdn_main, BF, tn=1024,
                           **proj_args).reshape(b, s, gdn_main)
            w_tail = _row_window(gdn_w_in_t, j, gdn_main, 2 * n_vh)
            w_ba = jnp.pad(w_tail[:2 * n_vh], ((0, LANES - 2 * n_vh), (0, 0)))[None]
            ba = _matmul([hn], w_ba, 0, 0, LANES, F32, **proj_args).reshape(b, s, LANES)
            xq = _matmul([hn], w_tail[None, 2 * n_vh:], 0, 0, d, BF, tn=1024, **proj_args)
            qkv, bg = _gdn_prep(proj, ba, gdn_conv_w[j], gdn_a_log[j], gdn_dt_bias[j], d_qk, d_v)
            y = _gdn_delta(qkv, proj, 2 * d_qk + d_v, bg, gdn_norm_g[j], d_qk, d_v)
            w_out = gdn_w_out
        xo = _xattn(xq.reshape(b, s, d), kv)
        y2 = y.reshape(m, y.shape[-1])
        h, hb = _matmul([y2, xo.reshape(m, d)], w_out, j, 0, d, F32, residual=h, emit_bf16=True)
        act = _ffn_in(hb, ffn_w_in, i, norm_ffn_g[i])
        last = i + 1 == depth
        out = _matmul([act], ffn_w_out, i, 0, d, F32, residual=h, emit_bf16=not last)
        h, hb = (out, None) if last else out
    return _rmsnorm(h, final_norm_g, F32).reshape(b, s, d)
```
